```python
import jax, jax.numpy as jnp
from jax import lax
import numpy as np

D_MODEL = 1024
BATCH = 16
SEQ = 4096
DEPTH = 2

CHUNK = 64
HEAD_DIM = 64
A_HEADS = 4
A_LEFT_CHUNKS = 8
A_MAX_REL = 128
B_HEADS = 4
C_HEADS = 4
IDX_HEADS = 4
IDX_DIM = 64
TOPK_MAX = 256
N_BRANCH = 3
MIX_W = 256
D_FF = 2816
CONV_W = 3
Q_BLOCK = 128
ROPE_THETA = 10000.0
EPS = 1e-6
NEG = -1e30

COL_SIZES = (MIX_W, MIX_W, MIX_W,
             MIX_W, MIX_W, MIX_W, B_HEADS,
             MIX_W, HEAD_DIM, HEAD_DIM,
             IDX_HEADS * IDX_DIM, IDX_DIM, IDX_HEADS,
             N_BRANCH * D_MODEL)
N_IN = sum(COL_SIZES)

kernel_name = "hybrid_streaming_encoder"


def _rmsnorm(x, g):
    xf = x.astype(jnp.float32)
    y = xf * lax.rsqrt(jnp.mean(xf * xf, axis=-1, keepdims=True) + EPS)
    return (y * g.astype(jnp.float32)).astype(x.dtype)


def _rope(x, cos, sin):
    xf = x.astype(jnp.float32)
    x1, x2 = jnp.split(xf, 2, axis=-1)
    c = cos[None, :, None, :]
    s = sin[None, :, None, :]
    return jnp.concatenate([x1 * c - x2 * s, x1 * s + x2 * c], axis=-1).astype(x.dtype)


def _blocks_to_seq(out):
    nb, b, blk, h, d = out.shape
    return out.transpose(1, 0, 2, 3, 4).reshape(b, nb * blk, h, d)


def _chunk_band_attention(q, k, v, rel_table):
    S = q.shape[1]
    dh = q.shape[-1]
    n_chunks = S // CHUNK
    pad = A_LEFT_CHUNKS * CHUNK
    band = pad + CHUNK
    kp = jnp.pad(k, ((0, 0), (pad, 0), (0, 0), (0, 0)))
    vp = jnp.pad(v, ((0, 0), (pad, 0), (0, 0), (0, 0)))
    qi = jnp.arange(CHUNK)
    kj = jnp.arange(band)
    rel = qi[:, None] + pad - kj[None, :]
    bias = rel_table[:, jnp.clip(rel, -A_MAX_REL, A_MAX_REL) + A_MAX_REL].astype(jnp.float32)
    scale = dh ** -0.5

    def one_chunk(n):
        start = n * CHUNK
        qc = lax.dynamic_slice_in_dim(q, start, CHUNK, axis=1)
        kc = lax.dynamic_slice_in_dim(kp, start, band, axis=1)
        vc = lax.dynamic_slice_in_dim(vp, start, band, axis=1)
        s = jnp.einsum('bqhd,bkhd->bhqk', qc, kc).astype(jnp.float32) * scale + bias
        valid = (start - pad + kj) >= 0
        s = jnp.where(valid[None, None, None, :], s, NEG)
        p = jax.nn.softmax(s, axis=-1).astype(v.dtype)
        return jnp.einsum('bhqk,bkhd->bqhd', p, vc)

    return _blocks_to_seq(lax.map(one_chunk, jnp.arange(n_chunks)))


def _forgetting_attention(q, k, v, log_f):
    S = q.shape[1]
    dh = q.shape[-1]
    cum = jnp.cumsum(log_f, axis=1).transpose(0, 2, 1)
    kpos = jnp.arange(S)
    scale = dh ** -0.5

    def one_block(i):
        start = i * Q_BLOCK
        qb = lax.dynamic_slice_in_dim(q, start, Q_BLOCK, axis=1)
        cq = lax.dynamic_slice_in_dim(cum, start, Q_BLOCK, axis=2)
        s = jnp.einsum('bqhd,bkhd->bhqk', qb, k).astype(jnp.float32) * scale
        s = s + cq[..., :, None] - cum[:, :, None, :]
        qpos = start + jnp.arange(Q_BLOCK)
        s = jnp.where((kpos[None, :] <= qpos[:, None])[None, None], s, NEG)
        p = jax.nn.softmax(s, axis=-1).astype(v.dtype)
        return jnp.einsum('bhqk,bkhd->bqhd', p, v)

    return _blocks_to_seq(lax.map(one_block, jnp.arange(S // Q_BLOCK)))


def _indexed_sparse_attention(q, k, v, q_idx, k_idx, w_idx):
    S = q.shape[1]
    dh = q.shape[-1]
    topk = min(TOPK_MAX, S // 4)
    chunk_k = jnp.arange(S) // CHUNK
    scale = dh ** -0.5

    def one_block(i):
        start = i * Q_BLOCK
        qb = lax.dynamic_slice_in_dim(q, start, Q_BLOCK, axis=1)
        qib = lax.dynamic_slice_in_dim(q_idx, start, Q_BLOCK, axis=1)
        wb = lax.dynamic_slice_in_dim(w_idx, start, Q_BLOCK, axis=1).astype(jnp.float32)
        logits = jnp.einsum('bqhd,bkd->bqhk', qib, k_idx).astype(jnp.float32)
        score = jnp.einsum('bqh,bqhk->bqk', wb, jax.nn.relu(logits))
        chunk_q = (start + jnp.arange(Q_BLOCK)) // CHUNK
        adm = chunk_k[None, :] <= chunk_q[:, None]
        score = jnp.where(adm[None], score, NEG)
        _, sel = lax.top_k(score, topk)
        sel_ok = chunk_k[sel] <= chunk_q[None, :, None]
        k_sel = jax.vmap(lambda kk, ii: kk[ii])(k, sel)
        v_sel = jax.vmap(lambda vv, ii: vv[ii])(v, sel)
        s = jnp.einsum('bqhd,bqkd->bhqk', qb, k_sel).astype(jnp.float32) * scale
        s = jnp.where(sel_ok[:, None], s, NEG)
        p = jax.nn.softmax(s, axis=-1).astype(v.dtype)
        return jnp.einsum('bhqk,bqkd->bqhd', p, v_sel)

    return _blocks_to_seq(lax.map(one_block, jnp.arange(S // Q_BLOCK)))


def _hybrid_mixer(h, w_in, b_gate, rel_table, b_forget, w_branch, w_out, cos, sin):
    B_, S, _ = h.shape
    z = h @ w_in
    offs = np.cumsum(np.array(COL_SIZES))[:-1].tolist()
    (qa, ka, va, qb, kb, vb, fb, qc, kc, vc, qi, ki, wi, zg) = jnp.split(z, offs, axis=-1)
    heads = lambda t, n: t.reshape(B_, S, n, -1)

    ya = _chunk_band_attention(heads(qa, A_HEADS), heads(ka, A_HEADS), heads(va, A_HEADS), rel_table)

    log_f = jax.nn.log_sigmoid(fb.astype(jnp.float32) + b_forget.astype(jnp.float32))
    yb = _forgetting_attention(heads(qb, B_HEADS), heads(kb, B_HEADS), heads(vb, B_HEADS), log_f)

    qc_r = _rope(heads(qc, C_HEADS), cos, sin)
    kc_r = _rope(kc[:, :, None, :], cos, sin)[:, :, 0, :]
    qi_r = _rope(heads(qi, IDX_HEADS), cos, sin) * (IDX_DIM ** -0.5)
    ki_r = _rope(ki[:, :, None, :], cos, sin)[:, :, 0, :]
    wi_s = wi * (IDX_HEADS ** -0.5)
    yc = _indexed_sparse_attention(qc_r, kc_r, vc, qi_r, ki_r, wi_s)

    gates = jax.nn.sigmoid(zg.reshape(B_, S, N_BRANCH, D_MODEL) + b_gate)
    merged = (gates[:, :, 0] * (ya.reshape(B_, S, MIX_W) @ w_branch[0])
              + gates[:, :, 1] * (yb.reshape(B_, S, MIX_W) @ w_branch[1])
              + gates[:, :, 2] * (yc.reshape(B_, S, MIX_W) @ w_branch[2]))
    return merged @ w_out


def _conv_ffn(h, w_up, conv_w, conv_b, w_down):
    a, g = jnp.split(h @ w_up, 2, axis=-1)
    a = lax.conv_general_dilated(a, conv_w[:, None, :], window_strides=(1,),
                                 padding=[(CONV_W - 1, 0)],
                                 dimension_numbers=('NWC', 'WIO', 'NWC'),
                                 feature_group_count=D_FF) + conv_b
    return (jax.nn.gelu(a) * g) @ w_down


def setup_inputs(seed: int = 0) -> dict:
    key = jax.random.key(seed)
    ks = jax.random.split(key, 16)
    f32 = jnp.float32
    nrm = lambda k, shape, s: jax.random.normal(k, shape, f32) * s
    return {
        'x': nrm(ks[0], (BATCH, SEQ, D_MODEL), 1.0),
        'c': nrm(ks[1], (BATCH, D_MODEL), 1.0),
        'w_ada': nrm(ks[2], (DEPTH, D_MODEL, 6 * D_MODEL), D_MODEL ** -0.5),
        'b_ada': nrm(ks[3], (DEPTH, 6 * D_MODEL), 0.01),
        'norm_g': 1.0 + nrm(ks[4], (DEPTH, 4, D_MODEL), 0.05),
        'w_in': nrm(ks[5], (DEPTH, D_MODEL, N_IN), D_MODEL ** -0.5),
        'b_gate': nrm(ks[6], (DEPTH, N_BRANCH, D_MODEL), 0.01),
        'rel_table': nrm(ks[7], (DEPTH, A_HEADS, 2 * A_MAX_REL + 1), 0.2),
        'b_forget': jax.random.uniform(ks[8], (DEPTH, B_HEADS), f32, 1.0, 4.0),
        'w_branch': nrm(ks[9], (DEPTH, N_BRANCH, MIX_W, D_MODEL), MIX_W ** -0.5),
        'w_out': nrm(ks[10], (DEPTH, D_MODEL, D_MODEL), D_MODEL ** -0.5),
        'w_up': nrm(ks[11], (DEPTH, D_MODEL, 2 * D_FF), D_MODEL ** -0.5),
        'conv_w': nrm(ks[12], (DEPTH, CONV_W, D_FF), CONV_W ** -0.5),
        'conv_b': nrm(ks[13], (DEPTH, D_FF), 0.01),
        'w_down': nrm(ks[14], (DEPTH, D_FF, D_MODEL), D_FF ** -0.5),
    }


def reference(x, c, w_ada, b_ada, norm_g, w_in, b_gate, rel_table, b_forget, w_branch, w_out,
              w_up, conv_w, conv_b, w_down):
    S = x.shape[1]
    pos = jnp.arange(S, dtype=jnp.float32)
    inv_freq = ROPE_THETA ** (-jnp.arange(0, HEAD_DIM, 2, dtype=jnp.float32) / HEAD_DIM)
    ang = pos[:, None] * inv_freq[None, :]
    cos, sin = jnp.cos(ang), jnp.sin(ang)
    c_act = jax.nn.silu(c)
    for l in range(DEPTH):
        mod = (c_act @ w_ada[l] + b_ada[l])[:, None, :]
        sh1, sc1, g1, sh2, sc2, g2 = jnp.split(mod, 6, axis=-1)
        h = _rmsnorm(x, norm_g[l, 0]) * (1.0 + sc1) + sh1
        y = _hybrid_mixer(h, w_in[l], b_gate[l], rel_table[l], b_forget[l], w_branch[l], w_out[l], cos, sin)
        x = x + g1 * _rmsnorm(y, norm_g[l, 1])
        h = _rmsnorm(x, norm_g[l, 2]) * (1.0 + sc2) + sh2
        y = _conv_ffn(h, w_up[l], conv_w[l], conv_b[l], w_down[l])
        x = x + g2 * _rmsnorm(y, norm_g[l, 3])
    return x
```

```python
import functools

import numpy as np
import jax
import jax.numpy as jnp
from jax import lax
from jax.experimental import pallas as pl
from jax.experimental.pallas import tpu as pltpu

F32 = jnp.float32
BF16 = jnp.bfloat16
I32 = jnp.int32

D_MODEL = 1024
CHUNK = 64
HEAD_DIM = 64
HEADS = 4
A_LEFT_CHUNKS = 8
A_MAX_REL = 128
IDX_DIM = 64
TOPK_MAX = 256
N_BRANCH = 3
MIX_W = 256
D_FF = 2816
CONV_W = 3
ROPE_THETA = 10000.0
EPS = 1e-6
NEG = -1e30

LANES = 128
SUBLANES = 8
VMEM_LIMIT = 56 * 1024 * 1024

_QA, _KA, _VA = 0, 512, 1024
_QB, _KB, _VB = 1280, 1536, 1792
_QC, _QI, _KK, _VC = 2048, 2304, 2560, 2688
_SM, _GT, _N_COLS = 2816, 2944, 6016

ROW_TILE = 512
A_TILE = 256
B_TILE = 256
C_TILE = 128
FF_TILE = 256

INT_MIN = -2 ** 31


def _sortable_key_of(value):
    bits = int(np.array(value, np.float32).view(np.int32))
    return bits ^ ((bits >> 31) & 0x7FFFFFFF)


KEY_NEG = _sortable_key_of(NEG)


def _dot(a, b):
    return jnp.dot(a, b, preferred_element_type=F32)


def _dot_nt(a, b):
    return lax.dot_general(a, b, (((1,), (1,)), ((), ())), preferred_element_type=F32)


def _params(n_axes):
    return pltpu.CompilerParams(dimension_semantics=("arbitrary",) * n_axes,
                                vmem_limit_bytes=VMEM_LIMIT)


def _resident(shape, index_map):
    return pl.BlockSpec(shape, index_map, pipeline_mode=pl.Buffered(1))


def _mod_kernel(c_ref, w_ref, b_ref, o_ref):
    c = c_ref[...]
    c_act = c * jax.nn.sigmoid(c)
    o_ref[0] = _dot(c_act.astype(BF16), w_ref[0].astype(BF16)) + b_ref[0]


def _modulation(c, w_ada, b_ada):
    depth, d, n = w_ada.shape
    b = c.shape[0]
    return pl.pallas_call(
        _mod_kernel,
        grid=(depth, n // d),
        in_specs=[pl.BlockSpec((b, d), lambda l, j: (0, 0)),
                  pl.BlockSpec((1, d, d), lambda l, j: (l, 0, j)),
                  pl.BlockSpec((1, 1, d), lambda l, j: (l, 0, j))],
        out_specs=pl.BlockSpec((1, b, d), lambda l, j: (l, 0, j)),
        out_shape=jax.ShapeDtypeStruct((depth, b, n), F32),
        compiler_params=_params(2),
        name="adaln_mod",
    )(c, w_ada, b_ada.reshape(depth, 1, n))


def _modulated_norm(x, g, sc, sh):
    ms = jnp.mean(x * x, axis=-1, keepdims=True)
    return (x * lax.rsqrt(ms + EPS) * g) * (1.0 + sc) + sh


def _in_kernel(x_ref, g_ref, sc_ref, sh_ref, w_ref, bg_ref, cos_ref, sin_ref,
               oa_ref, ob_ref, oc_ref, os_ref, og_ref):
    hb = _modulated_norm(x_ref[...], g_ref[...], sc_ref[0], sh_ref[0]).astype(BF16)
    proj = lambda lo, hi: _dot(hb, w_ref[:, lo:hi])
    oa_ref[...] = proj(_QA, _QB).astype(BF16)
    ob_ref[...] = proj(_QB, _QC).astype(BF16)
    zr = proj(_QC, _VC)
    lane = lax.broadcasted_iota(I32, (zr.shape[0], LANES), 1)
    first_half = (lane % HEAD_DIM) < (HEAD_DIM // 2)
    cos = cos_ref[...]
    sin = sin_ref[...]
    for s in range((_VC - _QC) // LANES):
        xs = zr[:, s * LANES:(s + 1) * LANES]
        swapped = jnp.where(first_half, pltpu.roll(xs, LANES - HEAD_DIM // 2, 1),
                            pltpu.roll(xs, HEAD_DIM // 2, 1))
        oc_ref[:, s * LANES:(s + 1) * LANES] = (xs * cos + swapped * sin).astype(BF16)
    oc_ref[:, _VC - _QC:_SM - _QC] = proj(_VC, _SM).astype(BF16)
    os_ref[...] = proj(_SM, _GT)
    og_ref[...] = jax.nn.sigmoid(proj(_GT, _N_COLS) + bg_ref[...]).astype(BF16)


def _in_projection(x2, g, sc, sh, w, b_gate, cos_t, sin_t, seq):
    t, d = x2.shape
    tm = min(ROW_TILE, seq)
    per_seq = seq // tm
    row = lambda i: (i, 0)
    const = lambda i: (0, 0)
    by_batch = lambda i: (i // per_seq, 0, 0)
    by_pos = lambda i: (i % per_seq, 0)
    widths = (_QB - _QA, _QC - _QB, _SM - _QC, _GT - _SM, _N_COLS - _GT)
    dtypes = (BF16, BF16, BF16, F32, BF16)
    return pl.pallas_call(
        _in_kernel,
        grid=(t // tm,),
        in_specs=[pl.BlockSpec((tm, d), row),
                  pl.BlockSpec((1, d), const),
                  pl.BlockSpec((1, 1, d), by_batch),
                  pl.BlockSpec((1, 1, d), by_batch),
                  _resident((d, _N_COLS), const),
                  pl.BlockSpec((1, _N_COLS - _GT), const),
                  pl.BlockSpec((tm, LANES), by_pos),
                  pl.BlockSpec((tm, LANES), by_pos)],
        out_specs=[pl.BlockSpec((tm, wd), row) for wd in widths],
        out_shape=[jax.ShapeDtypeStruct((t, wd), dt) for wd, dt in zip(widths, dtypes)],
        compiler_params=_params(1),
        name="in_projection",
    )(x2, g, sc, sh, w, b_gate, cos_t, sin_t)


def _layout_w_in(w_in):
    sizes = (MIX_W,) * 6 + (HEADS, MIX_W, HEAD_DIM, HEAD_DIM, HEADS * IDX_DIM, IDX_DIM, HEADS,
                            N_BRANCH * D_MODEL)
    offs = np.cumsum((0,) + sizes)
    qa, ka, va, qb, kb, vb, fb, qc, kc, vc, qi, ki, wi, zg = (
        w_in[:, offs[n]:offs[n + 1]] for n in range(len(sizes)))
    d = w_in.shape[0]
    scale = HEAD_DIM ** -0.5

    def pad_heads(w):
        w = w.reshape(d, HEADS, HEAD_DIM)
        return jnp.pad(w, ((0, 0), (0, 0), (0, LANES - HEAD_DIM))).reshape(d, HEADS * LANES)

    zeros = lambda n: jnp.zeros((d, n), w_in.dtype)
    cols = [pad_heads(qa * scale), pad_heads(ka), va,
            qb * scale, kb, vb,
            qc * scale, qi * (IDX_DIM ** -0.5), kc, ki,
            vc, zeros(LANES - HEAD_DIM),
            fb, wi * (HEADS ** -0.5), zeros(LANES - 2 * HEADS),
            zg]
    w = jnp.concatenate(cols, axis=1)
    assert w.shape[1] == _N_COLS
    return w.astype(BF16)


def _rope_tables(seq):
    pos = jnp.arange(seq, dtype=F32)
    inv_freq = ROPE_THETA ** (-jnp.arange(0, HEAD_DIM, 2, dtype=F32) / HEAD_DIM)
    ang = pos[:, None] * inv_freq[None, :]
    cos, sin = jnp.cos(ang), jnp.sin(ang)
    reps = LANES // HEAD_DIM
    cos_t = jnp.tile(jnp.concatenate([cos, cos], axis=1), (1, reps))
    sin_t = jnp.tile(jnp.concatenate([-sin, sin], axis=1), (1, reps))
    return cos_t, sin_t


def _cum_kernel(f_ref, b_ref, o_ref):
    z = f_ref[...] + b_ref[...]
    c = jnp.minimum(z, 0.0) - jnp.log1p(jnp.exp(-jnp.abs(z)))
    n = c.shape[1]
    lane = lax.broadcasted_iota(I32, c.shape, 1)
    shift = 1
    while shift < n:
        c = c + jnp.where(lane >= shift, pltpu.roll(c, shift, 1), 0.0)
        shift *= 2
    hi = c.astype(BF16)
    r1 = c - hi.astype(F32)
    mid = r1.astype(BF16)
    lo = (r1 - mid.astype(F32)).astype(BF16)
    o_ref[0] = hi
    o_ref[1] = mid
    o_ref[2] = lo


def _forget_prefix(f_rows, b_col):
    r, s = f_rows.shape
    return pl.pallas_call(
        _cum_kernel,
        grid=(1,),
        in_specs=[pl.BlockSpec((r, s), lambda i: (0, 0)),
                  pl.BlockSpec((r, 1), lambda i: (0, 0))],
        out_specs=pl.BlockSpec((3, r, s), lambda i: (0, 0, 0)),
        out_shape=jax.ShapeDtypeStruct((3, r, s), BF16),
        compiler_params=_params(1),
        name="forget_prefix",
    )(f_rows, b_col)


def _merge_head_pairs(o_ref, outs):
    lane = lax.broadcasted_iota(I32, outs[0].shape, 1)
    for p in range(HEADS // 2):
        o_ref[:, p * LANES:(p + 1) * LANES] = jnp.where(
            lane < HEAD_DIM, outs[2 * p], outs[2 * p + 1]).astype(o_ref.dtype)


def _band_kernel(q_ref, k0_ref, k1_ref, k2_ref, v0_ref, v1_ref, v2_ref, bias_ref, o_ref):
    i = pl.program_id(1)
    k_refs = (k0_ref, k1_ref, k2_ref)
    v_refs = (v0_ref, v1_ref, v2_ref)
    n_kb = len(k_refs)
    outs = []
    for h in range(HEADS):
        hs = slice(h * LANES, (h + 1) * LANES)
        vs = slice((h // 2) * LANES, (h // 2 + 1) * LANES)
        q = q_ref[:, hs]
        s = []
        for j in range(n_kb):
            sj = _dot_nt(q, k_refs[j][:, hs]) + bias_ref[h, :, j * A_TILE:(j + 1) * A_TILE]
            if j < n_kb - 1:
                sj = jnp.where(i - (n_kb - 1) + j >= 0, sj, NEG)
            s.append(sj)
        m = functools.reduce(jnp.maximum, [sj.max(axis=1, keepdims=True) for sj in s])
        p = [jnp.exp(sj - m) for sj in s]
        l = functools.reduce(jnp.add, [pj.sum(axis=1, keepdims=True) for pj in p])
        acc = functools.reduce(jnp.add, [_dot(p[j].astype(BF16), v_refs[j][:, vs])
                                         for j in range(n_kb)])
        outs.append(acc / l)
    _merge_head_pairs(o_ref, outs)


def _band_bias(rel_table):
    n_kb = A_LEFT_CHUNKS * CHUNK // A_TILE + 1
    qi = np.arange(A_TILE)[:, None]
    kj = np.arange(n_kb * A_TILE)[None, :]
    rel = qi + (n_kb - 1) * A_TILE - kj
    dc = qi // CHUNK + (n_kb - 1) * A_TILE // CHUNK - kj // CHUNK
    band = (dc >= 0) & (dc <= A_LEFT_CHUNKS)
    bias = rel_table[:, np.clip(rel, -A_MAX_REL, A_MAX_REL) + A_MAX_REL].astype(F32)
    return jnp.where(jnp.asarray(band)[None], bias, NEG)


def _band_attention(oa, bias, batch, seq):
    t = oa.shape[0]
    nq = seq // A_TILE
    n_kb = bias.shape[2] // A_TILE
    qw = HEADS * LANES

    def k_spec(j, width, col):
        return pl.BlockSpec(
            (A_TILE, width),
            lambda b, i: (b * nq + jnp.maximum(i - (n_kb - 1) + j, 0), col))

    return pl.pallas_call(
        _band_kernel,
        grid=(batch, nq),
        in_specs=([pl.BlockSpec((A_TILE, qw), lambda b, i: (b * nq + i, _QA // qw))]
                  + [k_spec(j, qw, _KA // qw) for j in range(n_kb)]
                  + [k_spec(j, MIX_W, _VA // MIX_W) for j in range(n_kb)]
                  + [pl.BlockSpec(bias.shape, lambda b, i: (0, 0, 0))]),
        out_specs=pl.BlockSpec((A_TILE, MIX_W), lambda b, i: (b * nq + i, 0)),
        out_shape=jax.ShapeDtypeStruct((t, MIX_W), BF16),
        compiler_params=_params(2),
        name="band_attention",
    )(oa, *([oa] * (2 * n_kb)), bias)


def _fox_kernel(q_ref, k_ref, v_ref, o_ref):
    i = pl.program_id(1)
    tq = q_ref.shape[0]
    row = lax.broadcasted_iota(I32, (tq, B_TILE), 0)
    col = lax.broadcasted_iota(I32, (tq, B_TILE), 1)
    outs = []
    for h in range(HEADS):
        hs = slice(h * LANES, (h + 1) * LANES)
        vs = slice((h // 2) * LANES, (h // 2 + 1) * LANES)
        q = q_ref[:, hs]

        def step(j, carry, diagonal):
            m, l, acc = carry
            rows = pl.ds(pl.multiple_of(j * B_TILE, B_TILE), B_TILE)
            s = _dot_nt(q, k_ref[rows, hs])
            if diagonal:
                s = jnp.where(col <= row, s, NEG)
            m_new = jnp.maximum(m, s.max(axis=1, keepdims=True))
            alpha = jnp.exp(m - m_new)
            p = jnp.exp(s - m_new)
            l = alpha * l + p.sum(axis=1, keepdims=True)
            acc = alpha * acc + _dot(p.astype(BF16), v_ref[rows, vs])
            return m_new, l, acc

        init = (jnp.full((tq, 1), NEG, F32), jnp.zeros((tq, 1), F32), jnp.zeros((tq, LANES), F32))
        carry = lax.fori_loop(0, i, lambda j, c: step(j, c, False), init)
        _, l, acc = step(i, carry, True)
        outs.append(acc / l)
    _merge_head_pairs(o_ref, outs)


def _fox_attention(q_aug, k_aug, ob, batch, seq):
    t = q_aug.shape[0]
    nq = seq // B_TILE
    qw = HEADS * LANES
    return pl.pallas_call(
        _fox_kernel,
        grid=(batch, nq),
        in_specs=[pl.BlockSpec((B_TILE, qw), lambda b, i: (b * nq + i, 0)),
                  pl.BlockSpec((seq, qw), lambda b, i: (b, 0)),
                  pl.BlockSpec((seq, MIX_W), lambda b, i: (b, (_VB - _QB) // MIX_W))],
        out_specs=pl.BlockSpec((B_TILE, MIX_W), lambda b, i: (b * nq + i, 0)),
        out_shape=jax.ShapeDtypeStruct((t, MIX_W), BF16),
        compiler_params=_params(2),
        name="fox_attention",
    )(q_aug, k_aug, ob)


def _fox_operands(ob, cum_planes, batch, seq):
    t = batch * seq
    planes = cum_planes.reshape(3, batch, HEADS, seq).transpose(1, 3, 2, 0).reshape(t, HEADS, 3)
    ones = jnp.ones_like(planes)
    fill = jnp.zeros((t, HEADS, LANES - HEAD_DIM - 6), BF16)
    q = ob[:, 0:MIX_W].reshape(t, HEADS, HEAD_DIM)
    k = ob[:, MIX_W:2 * MIX_W].reshape(t, HEADS, HEAD_DIM)
    q_aug = jnp.concatenate([q, planes, ones, fill], axis=-1).reshape(t, HEADS * LANES)
    k_aug = jnp.concatenate([k, ones, -planes, fill], axis=-1).reshape(t, HEADS * LANES)
    return q_aug, k_aug


def _sparse_kernel(kk_ref, wi_ref, wc_ref, w_ref, vt_ref, o_ref, key_scr, bias_scr, *, topk):
    i = pl.program_id(1)
    n_blk = i + 1
    kb = C_TILE
    row = lax.broadcasted_iota(I32, (kb, LANES), 0)
    lane = lax.broadcasted_iota(I32, (kb, LANES), 1)
    limit = i * C_TILE + CHUNK * (lane // CHUNK + 1)
    w = w_ref[0, 0]
    rows_of = lambda j: pl.ds(pl.multiple_of(j * kb, kb), kb)

    def score_block(j, carry):
        logits = _dot(kk_ref[rows_of(j), :], wi_ref[0, 0])
        score = functools.reduce(jnp.add, [
            w[h:h + 1, :] * jnp.maximum(logits[:, h * LANES:(h + 1) * LANES], 0.0)
            for h in range(HEADS)])
        bits = pltpu.bitcast(score + 0.0, I32)
        key = bits ^ ((bits >> 31) & 0x7FFFFFFF)
        key_scr[rows_of(j), :] = jnp.where(j * kb + row < limit, key, KEY_NEG)
        return carry

    lax.fori_loop(0, n_blk, score_block, 0)

    def count(pred):
        def body(j, cnt):
            return cnt + jnp.sum(pred(key_scr[rows_of(j), :]).astype(I32), axis=0, keepdims=True)
        return lax.fori_loop(0, n_blk, body, jnp.zeros((1, LANES), I32))

    def bit_step(it, thr):
        cand = thr + lax.shift_left(jnp.int32(1), 31 - it)
        return jnp.where(count(lambda blk: blk >= cand) >= topk, cand, thr)

    thr = lax.fori_loop(0, 32, bit_step, jnp.full((1, LANES), INT_MIN, I32))

    room = (topk - count(lambda blk: blk > thr)).astype(F32)
    strict_lower = (lax.broadcasted_iota(I32, (kb, kb), 1)
                    < lax.broadcasted_iota(I32, (kb, kb), 0)).astype(BF16)

    def select_block(j, ties_before):
        blk = key_scr[rows_of(j), :]
        tie = jnp.where(blk == thr, 1.0, 0.0)
        rank = _dot(strict_lower, tie.astype(BF16)) + ties_before
        sel = (j * kb + row < limit) & ((blk > thr) | ((blk == thr) & (rank < room)))
        bias_scr[rows_of(j), :] = jnp.where(sel, 0.0, NEG)
        return ties_before + jnp.sum(tie, axis=0, keepdims=True)

    lax.fori_loop(0, n_blk, select_block, jnp.zeros((1, LANES), F32))

    def attend_block(j, carry):
        s_all = _dot(kk_ref[rows_of(j), :], wc_ref[0, 0])
        bias = bias_scr[rows_of(j), :]
        vt = vt_ref[0, j]
        new = []
        for h in range(HEADS):
            m, l, acc = carry[h]
            s = s_all[:, h * LANES:(h + 1) * LANES] + bias
            m_new = jnp.maximum(m, s.max(axis=0, keepdims=True))
            alpha = jnp.exp(m - m_new)
            p = jnp.exp(s - m_new)
            l = alpha * l + p.sum(axis=0, keepdims=True)
            acc = alpha * acc + _dot(vt, p.astype(BF16))
            new.append((m_new, l, acc))
        return tuple(new)

    init = tuple((jnp.full((1, LANES), NEG, F32), jnp.zeros((1, LANES), F32),
                  jnp.zeros((HEAD_DIM, LANES), F32)) for _ in range(HEADS))
    final = lax.fori_loop(0, n_blk, attend_block, init)
    out_t = jnp.concatenate([acc / l for _, l, acc in final], axis=0)
    o_ref[...] = out_t.T.astype(o_ref.dtype)


def _sparse_attention(oc, wi_t, wc_t, w_rows, v_t, batch, seq, topk):
    t = oc.shape[0]
    nq = seq // C_TILE
    blk4 = lambda b, i: (b, i, 0, 0)
    return pl.pallas_call(
        functools.partial(_sparse_kernel, topk=topk),
        grid=(batch, nq),
        in_specs=[pl.BlockSpec((seq, LANES), lambda b, i: (b, (_KK - _QC) // LANES)),
                  pl.BlockSpec((1, 1, LANES, HEADS * C_TILE), blk4),
                  pl.BlockSpec((1, 1, LANES, HEADS * C_TILE), blk4),
                  pl.BlockSpec((1, 1, HEADS, C_TILE), blk4),
                  pl.BlockSpec((1, nq, HEAD_DIM, C_TILE), lambda b, i: (b, 0, 0, 0))],
        out_specs=pl.BlockSpec((C_TILE, MIX_W), lambda b, i: (b * nq + i, 0)),
        out_shape=jax.ShapeDtypeStruct((t, MIX_W), BF16),
        scratch_shapes=[pltpu.VMEM((seq, LANES), I32), pltpu.VMEM((seq, LANES), F32)],
        compiler_params=_params(2),
        name="sparse_attention",
    )(oc, wi_t, wc_t, w_rows, v_t)


def _sparse_operands(oc, small, batch, seq):
    nq = seq // C_TILE

    def transposed(cols):
        q = cols.reshape(batch, nq, C_TILE, HEADS, HEAD_DIM)
        return q.transpose(0, 1, 4, 3, 2).reshape(batch, nq, HEAD_DIM, HEADS * C_TILE)

    zeros = jnp.zeros((batch, nq, HEAD_DIM, HEADS * C_TILE), BF16)
    wc_t = jnp.concatenate([transposed(oc[:, 0:MIX_W]), zeros], axis=2)
    wi_t = jnp.concatenate([zeros, transposed(oc[:, MIX_W:2 * MIX_W])], axis=2)
    w_rows = small[:, HEADS:2 * HEADS].reshape(batch, nq, C_TILE, HEADS).transpose(0, 1, 3, 2)
    v = oc[:, _VC - _QC:_VC - _QC + HEAD_DIM]
    v_t = v.reshape(batch, nq, C_TILE, HEAD_DIM).transpose(0, 1, 3, 2)
    return wi_t, wc_t, w_rows, v_t


def _rms_scale(y, g):
    return y * lax.rsqrt(jnp.mean(y * y, axis=-1, keepdims=True) + EPS) * g


def _merge_kernel(ya_ref, yb_ref, yc_ref, gt_ref, x_ref, wb_ref, wo_ref, g_ref, gate_ref, o_ref):
    d = x_ref.shape[1]
    merged = functools.reduce(jnp.add, [
        gt_ref[:, n * d:(n + 1) * d].astype(F32) * _dot(y_ref[...], wb_ref[n])
        for n, y_ref in enumerate((ya_ref, yb_ref, yc_ref))])
    y = _dot(merged.astype(BF16), wo_ref[...])
    o_ref[...] = x_ref[...] + gate_ref[0] * _rms_scale(y, g_ref[...])


def _merge_projection(ya, yb, yc, gates, x2, w_branch, w_out, g, gate, seq):
    t, d = x2.shape
    tm = min(ROW_TILE, seq)
    per_seq = seq // tm
    row = lambda i: (i, 0)
    return pl.pallas_call(
        _merge_kernel,
        grid=(t // tm,),
        in_specs=[pl.BlockSpec((tm, MIX_W), row)] * 3
        + [pl.BlockSpec((tm, N_BRANCH * d), row),
           pl.BlockSpec((tm, d), row),
           _resident((N_BRANCH, MIX_W, d), lambda i: (0, 0, 0)),
           _resident((d, d), lambda i: (0, 0)),
           pl.BlockSpec((1, d), lambda i: (0, 0)),
           pl.BlockSpec((1, 1, d), lambda i: (i // per_seq, 0, 0))],
        out_specs=pl.BlockSpec((tm, d), row),
        out_shape=jax.ShapeDtypeStruct((t, d), F32),
        compiler_params=_params(1),
        name="merge_projection",
    )(ya, yb, yc, gates, x2, w_branch, w_out, g, gate)


def _ffn_kernel(x_ref, g_ref, sc_ref, sh_ref, wu_ref, cw_ref, cb_ref, wd_ref, g2_ref, gate_ref,
                o_ref, a_scr, tail_scr, *, per_seq):
    i = pl.program_id(0)
    tm = x_ref.shape[0]
    halo = SUBLANES
    x = x_ref[...]
    hb = _modulated_norm(x, g_ref[...], sc_ref[0], sh_ref[0]).astype(BF16)
    seq_start = (i % per_seq) == 0
    acc = jnp.zeros(x.shape, F32)
    for c in range(D_FF // FF_TILE):
        cs = slice(c * FF_TILE, (c + 1) * FF_TILE)
        a = _dot(hb, wu_ref[:, cs])
        gate = _dot(hb, wu_ref[:, D_FF + c * FF_TILE:D_FF + (c + 1) * FF_TILE])
        a_scr[0:halo, :] = jnp.where(seq_start, 0.0, tail_scr[:, cs])
        a_scr[halo:halo + tm, :] = a
        tail_scr[:, cs] = a[tm - halo:tm, :]
        conv = (cw_ref[0:1, cs] * a_scr[halo - 2:halo - 2 + tm, :]
                + cw_ref[1:2, cs] * a_scr[halo - 1:halo - 1 + tm, :]
                + cw_ref[2:3, cs] * a + cb_ref[:, cs])
        acc = acc + _dot((jax.nn.gelu(conv, approximate=True) * gate).astype(BF16), wd_ref[cs, :])
    o_ref[...] = x + gate_ref[0] * _rms_scale(acc, g2_ref[...])


def _conv_ffn(x2, g, sc, sh, w_up, conv_w, conv_b, w_down, g2, gate, seq):
    t, d = x2.shape
    tm = min(ROW_TILE, seq)
    per_seq = seq // tm
    row = lambda i: (i, 0)
    const = lambda i: (0, 0)
    by_batch = lambda i: (i // per_seq, 0, 0)
    return pl.pallas_call(
        functools.partial(_ffn_kernel, per_seq=per_seq),
        grid=(t // tm,),
        in_specs=[pl.BlockSpec((tm, d), row),
                  pl.BlockSpec((1, d), const),
                  pl.BlockSpec((1, 1, d), by_batch),
                  pl.BlockSpec((1, 1, d), by_batch),
                  _resident((d, 2 * D_FF), const),
                  pl.BlockSpec((CONV_W, D_FF), const),
                  pl.BlockSpec((1, D_FF), const),
                  _resident((D_FF, d), const),
                  pl.BlockSpec((1, d), const),
                  pl.BlockSpec((1, 1, d), by_batch)],
        out_specs=pl.BlockSpec((tm, d), row),
        out_shape=jax.ShapeDtypeStruct((t, d), F32),
        scratch_shapes=[pltpu.VMEM((tm + SUBLANES, FF_TILE), F32),
                        pltpu.VMEM((SUBLANES, D_FF), F32)],
        compiler_params=_params(1),
        name="conv_ffn",
    )(x2, g, sc, sh, w_up, conv_w, conv_b, w_down, g2, gate)


def kernel(x, c, w_ada, b_ada, norm_g, w_in, b_gate, rel_table, b_forget, w_branch, w_out,
           w_up, conv_w, conv_b, w_down):
    batch, seq, d = x.shape
    depth = w_ada.shape[0]
    t = batch * seq
    topk = min(TOPK_MAX, seq // 4)
    cos_t, sin_t = _rope_tables(seq)
    mod = _modulation(c, w_ada, b_ada)
    x2 = x.reshape(t, d)
    for l in range(depth):
        sh1, sc1, g1, sh2, sc2, g2 = (mod[l, :, n * d:(n + 1) * d].reshape(batch, 1, d)
                                      for n in range(6))
        oa, ob, oc, small, gates = _in_projection(
            x2, norm_g[l, 0:1], sc1, sh1, _layout_w_in(w_in[l]),
            b_gate[l].reshape(1, N_BRANCH * d), cos_t, sin_t, seq)

        ya = _band_attention(oa, _band_bias(rel_table[l]), batch, seq)

        f_rows = small[:, 0:HEADS].reshape(batch, seq, HEADS).transpose(0, 2, 1)
        planes = _forget_prefix(f_rows.reshape(batch * HEADS, seq),
                                jnp.tile(b_forget[l], batch).reshape(batch * HEADS, 1))
        q_aug, k_aug = _fox_operands(ob, planes, batch, seq)
        yb = _fox_attention(q_aug, k_aug, ob, batch, seq)

        yc = _sparse_attention(oc, *_sparse_operands(oc, small, batch, seq), batch, seq, topk)

        x2 = _merge_projection(ya, yb, yc, gates, x2, w_branch[l].astype(BF16),
                               w_out[l].astype(BF16), norm_g[l, 1:2], g1, seq)
        x2 = _conv_ffn(x2, norm_g[l, 2:3], sc2, sh2, w_up[l].astype(BF16), conv_w[l],
                       conv_b[l].reshape(1, D_FF), w_down[l].astype(BF16), norm_g[l, 3:4], g2, seq)
    return x2.reshape(batch, seq, d)
```

```python
import functools

import numpy as np
import jax
import jax.numpy as jnp
from jax import lax
from jax.experimental import pallas as pl
from jax.experimental.pallas import tpu as pltpu

F32 = jnp.float32
BF16 = jnp.bfloat16
I32 = jnp.int32

D_MODEL = 1024
CHUNK = 64
HEAD_DIM = 64
HEADS = 4
A_LEFT_CHUNKS = 8
A_MAX_REL = 128
IDX_DIM = 64
TOPK_MAX = 256
N_BRANCH = 3
MIX_W = 256
D_FF = 2816
CONV_W = 3
ROPE_THETA = 10000.0
EPS = 1e-6
NEG = -1e30

LANES = 128
SUBLANES = 8
VMEM_LIMIT = 56 * 1024 * 1024

_QA, _KA, _VA = 0, 512, 1024
_QB, _KB, _VB = 1280, 1536, 1792
_QC, _QI, _KK, _VC = 2048, 2304, 2560, 2688
_SM, _GT, _N_COLS = 2816, 2944, 6016

ROW_TILE = 512
A_TILE = 256
B_TILE = 256
B_CHUNK = 512
C_TILE = 128
C_CHUNK = 512
FF_TILE = 256

PACKED_ROWS = 16
INT16_MIN = -2 ** 15
I16 = jnp.int16


def _sortable_key_of(value):
    bits = int(np.array(value, np.float32).view(np.int32))
    return bits ^ ((bits >> 31) & 0x7FFFFFFF)


KEY_NEG = _sortable_key_of(NEG)


def _dot(a, b):
    return jnp.dot(a, b, preferred_element_type=F32)


def _dot_nt(a, b):
    return lax.dot_general(a, b, (((1,), (1,)), ((), ())), preferred_element_type=F32)


def _params(n_axes):
    return pltpu.CompilerParams(dimension_semantics=("arbitrary",) * n_axes,
                                vmem_limit_bytes=VMEM_LIMIT)


def _resident(shape, index_map):
    return pl.BlockSpec(shape, index_map, pipeline_mode=pl.Buffered(1))


def _mod_kernel(c_ref, w_ref, b_ref, o_ref):
    c = c_ref[...]
    c_act = c * jax.nn.sigmoid(c)
    o_ref[0] = _dot(c_act.astype(BF16), w_ref[0].astype(BF16)) + b_ref[0]


def _modulation(c, w_ada, b_ada):
    depth, d, n = w_ada.shape
    b = c.shape[0]
    return pl.pallas_call(
        _mod_kernel,
        grid=(depth, n // d),
        in_specs=[pl.BlockSpec((b, d), lambda l, j: (0, 0)),
                  pl.BlockSpec((1, d, d), lambda l, j: (l, 0, j)),
                  pl.BlockSpec((1, 1, d), lambda l, j: (l, 0, j))],
        out_specs=pl.BlockSpec((1, b, d), lambda l, j: (l, 0, j)),
        out_shape=jax.ShapeDtypeStruct((depth, b, n), F32),
        compiler_params=_params(2),
        name="adaln_mod",
    )(c, w_ada, b_ada.reshape(depth, 1, n))


def _modulated_norm(x, g, sc, sh):
    ms = jnp.mean(x * x, axis=-1, keepdims=True)
    return (x * lax.rsqrt(ms + EPS) * g) * (1.0 + sc) + sh


def _in_kernel(x_ref, g_ref, sc_ref, sh_ref, w_ref, bg_ref, cos_ref, sin_ref,
               oa_ref, ob_ref, oc_ref, os_ref, og_ref):
    hb = _modulated_norm(x_ref[...], g_ref[...], sc_ref[0], sh_ref[0]).astype(BF16)
    proj = lambda lo, hi: _dot(hb, w_ref[:, lo:hi])
    oa_ref[...] = proj(_QA, _QB).astype(BF16)
    ob_ref[...] = proj(_QB, _QC).astype(BF16)
    zr = proj(_QC, _VC)
    lane = lax.broadcasted_iota(I32, (zr.shape[0], LANES), 1)
    first_half = (lane % HEAD_DIM) < (HEAD_DIM // 2)
    cos = cos_ref[...]
    sin = sin_ref[...]
    for s in range((_VC - _QC) // LANES):
        xs = zr[:, s * LANES:(s + 1) * LANES]
        swapped = jnp.where(first_half, pltpu.roll(xs, LANES - HEAD_DIM // 2, 1),
                            pltpu.roll(xs, HEAD_DIM // 2, 1))
        oc_ref[:, s * LANES:(s + 1) * LANES] = (xs * cos + swapped * sin).astype(BF16)
    oc_ref[:, _VC - _QC:_SM - _QC] = proj(_VC, _SM).astype(BF16)
    os_ref[...] = proj(_SM, _GT)
    og_ref[...] = jax.nn.sigmoid(proj(_GT, _N_COLS) + bg_ref[...]).astype(BF16)


def _in_projection(x2, g, sc, sh, w, b_gate, cos_t, sin_t, seq):
    t, d = x2.shape
    tm = min(ROW_TILE, seq)
    per_seq = seq // tm
    row = lambda i: (i, 0)
    const = lambda i: (0, 0)
    by_batch = lambda i: (i // per_seq, 0, 0)
    by_pos = lambda i: (i % per_seq, 0)
    widths = (_QB - _QA, _QC - _QB, _SM - _QC, _GT - _SM, _N_COLS - _GT)
    dtypes = (BF16, BF16, BF16, F32, BF16)
    return pl.pallas_call(
        _in_kernel,
        grid=(t // tm,),
        in_specs=[pl.BlockSpec((tm, d), row),
                  pl.BlockSpec((1, d), const),
                  pl.BlockSpec((1, 1, d), by_batch),
                  pl.BlockSpec((1, 1, d), by_batch),
                  _resident((d, _N_COLS), const),
                  pl.BlockSpec((1, _N_COLS - _GT), const),
                  pl.BlockSpec((tm, LANES), by_pos),
                  pl.BlockSpec((tm, LANES), by_pos)],
        out_specs=[pl.BlockSpec((tm, wd), row) for wd in widths],
        out_shape=[jax.ShapeDtypeStruct((t, wd), dt) for wd, dt in zip(widths, dtypes)],
        compiler_params=_params(1),
        name="in_projection",
    )(x2, g, sc, sh, w, b_gate, cos_t, sin_t)


def _layout_w_in(w_in):
    sizes = (MIX_W,) * 6 + (HEADS, MIX_W, HEAD_DIM, HEAD_DIM, HEADS * IDX_DIM, IDX_DIM, HEADS,
                            N_BRANCH * D_MODEL)
    offs = np.cumsum((0,) + sizes)
    qa, ka, va, qb, kb, vb, fb, qc, kc, vc, qi, ki, wi, zg = (
        w_in[:, offs[n]:offs[n + 1]] for n in range(len(sizes)))
    d = w_in.shape[0]
    scale = HEAD_DIM ** -0.5

    def pad_heads(w):
        w = w.reshape(d, HEADS, HEAD_DIM)
        return jnp.pad(w, ((0, 0), (0, 0), (0, LANES - HEAD_DIM))).reshape(d, HEADS * LANES)

    zeros = lambda n: jnp.zeros((d, n), w_in.dtype)
    cols = [pad_heads(qa * scale), pad_heads(ka), va,
            qb * scale, kb, vb,
            qc * scale, qi * (IDX_DIM ** -0.5), kc, ki,
            vc, zeros(LANES - HEAD_DIM),
            fb, wi * (HEADS ** -0.5), zeros(LANES - 2 * HEADS),
            zg]
    w = jnp.concatenate(cols, axis=1)
    assert w.shape[1] == _N_COLS
    return w.astype(BF16)


def _rope_tables(seq):
    pos = jnp.arange(seq, dtype=F32)
    inv_freq = ROPE_THETA ** (-jnp.arange(0, HEAD_DIM, 2, dtype=F32) / HEAD_DIM)
    ang = pos[:, None] * inv_freq[None, :]
    cos, sin = jnp.cos(ang), jnp.sin(ang)
    reps = LANES // HEAD_DIM
    cos_t = jnp.tile(jnp.concatenate([cos, cos], axis=1), (1, reps))
    sin_t = jnp.tile(jnp.concatenate([-sin, sin], axis=1), (1, reps))
    return cos_t, sin_t


def _cum_kernel(f_ref, b_ref, o_ref):
    z = f_ref[...] + b_ref[...]
    c = jnp.minimum(z, 0.0) - jnp.log1p(jnp.exp(-jnp.abs(z)))
    n = c.shape[1]
    lane = lax.broadcasted_iota(I32, c.shape, 1)
    shift = 1
    while shift < n:
        c = c + jnp.where(lane >= shift, pltpu.roll(c, shift, 1), 0.0)
        shift *= 2
    hi = c.astype(BF16)
    r1 = c - hi.astype(F32)
    mid = r1.astype(BF16)
    lo = (r1 - mid.astype(F32)).astype(BF16)
    o_ref[0] = hi
    o_ref[1] = mid
    o_ref[2] = lo


def _forget_prefix(f_rows, b_col):
    r, s = f_rows.shape
    return pl.pallas_call(
        _cum_kernel,
        grid=(1,),
        in_specs=[pl.BlockSpec((r, s), lambda i: (0, 0)),
                  pl.BlockSpec((r, 1), lambda i: (0, 0))],
        out_specs=pl.BlockSpec((3, r, s), lambda i: (0, 0, 0)),
        out_shape=jax.ShapeDtypeStruct((3, r, s), BF16),
        compiler_params=_params(1),
        name="forget_prefix",
    )(f_rows, b_col)


def _merge_head_pairs(o_ref, outs):
    lane = lax.broadcasted_iota(I32, outs[0].shape, 1)
    for p in range(HEADS // 2):
        o_ref[:, p * LANES:(p + 1) * LANES] = jnp.where(
            lane < HEAD_DIM, outs[2 * p], outs[2 * p + 1]).astype(o_ref.dtype)


def _band_kernel(q_ref, k0_ref, k1_ref, k2_ref, v0_ref, v1_ref, v2_ref, bias_ref, o_ref):
    i = pl.program_id(1)
    k_refs = (k0_ref, k1_ref, k2_ref)
    v_refs = (v0_ref, v1_ref, v2_ref)
    n_kb = len(k_refs)
    outs = []
    for h in range(HEADS):
        hs = slice(h * LANES, (h + 1) * LANES)
        vs = slice((h // 2) * LANES, (h // 2 + 1) * LANES)
        q = q_ref[:, hs]
        s = []
        for j in range(n_kb):
            sj = _dot_nt(q, k_refs[j][:, hs]) + bias_ref[h, :, j * A_TILE:(j + 1) * A_TILE]
            if j < n_kb - 1:
                sj = jnp.where(i - (n_kb - 1) + j >= 0, sj, NEG)
            s.append(sj)
        m = functools.reduce(jnp.maximum, [sj.max(axis=1, keepdims=True) for sj in s])
        p = [jnp.exp(sj - m) for sj in s]
        l = functools.reduce(jnp.add, [pj.sum(axis=1, keepdims=True) for pj in p])
        acc = functools.reduce(jnp.add, [_dot(p[j].astype(BF16), v_refs[j][:, vs])
                                         for j in range(n_kb)])
        outs.append(acc / l)
    _merge_head_pairs(o_ref, outs)


def _band_bias(rel_table):
    n_kb = A_LEFT_CHUNKS * CHUNK // A_TILE + 1
    width = n_kb * A_TILE
    heads = rel_table.shape[0]
    span = A_TILE + width - 1
    top = (n_kb - 1) * A_TILE + A_TILE - 1
    n_hi = top - A_MAX_REL
    n_lo = span - n_hi - (2 * A_MAX_REL + 1)
    f = jnp.concatenate([jnp.repeat(rel_table[:, :1], n_lo, axis=1), rel_table,
                         jnp.repeat(rel_table[:, -1:], n_hi, axis=1)], axis=1)
    g = jnp.pad(f[:, ::-1], ((0, 0), (0, 1)))
    period = span + 1
    skew = jnp.tile(g, (1, A_TILE + 1))[:, :A_TILE * (period + 1)]
    skew = skew.reshape(heads, A_TILE, period + 1)[:, :, :width]
    bias = skew[:, ::-1, :].astype(F32)
    qi = np.arange(A_TILE)[:, None]
    kj = np.arange(width)[None, :]
    dc = qi // CHUNK + (n_kb - 1) * A_TILE // CHUNK - kj // CHUNK
    band = (dc >= 0) & (dc <= A_LEFT_CHUNKS)
    return jnp.where(jnp.asarray(band)[None], bias, NEG)


def _band_attention(oa, bias, batch, seq):
    t = oa.shape[0]
    nq = seq // A_TILE
    n_kb = bias.shape[2] // A_TILE
    qw = HEADS * LANES

    def k_spec(j, width, col):
        return pl.BlockSpec(
            (A_TILE, width),
            lambda b, i: (b * nq + jnp.maximum(i - (n_kb - 1) + j, 0), col))

    return pl.pallas_call(
        _band_kernel,
        grid=(batch, nq),
        in_specs=([pl.BlockSpec((A_TILE, qw), lambda b, i: (b * nq + i, _QA // qw))]
                  + [k_spec(j, qw, _KA // qw) for j in range(n_kb)]
                  + [k_spec(j, MIX_W, _VA // MIX_W) for j in range(n_kb)]
                  + [pl.BlockSpec(bias.shape, lambda b, i: (0, 0, 0))]),
        out_specs=pl.BlockSpec((A_TILE, MIX_W), lambda b, i: (b * nq + i, 0)),
        out_shape=jax.ShapeDtypeStruct((t, MIX_W), BF16),
        compiler_params=_params(2),
        name="band_attention",
    )(oa, *([oa] * (2 * n_kb)), bias)


def _fold_rows(x, op):
    return op(x.reshape(x.shape[0] // SUBLANES, SUBLANES, x.shape[1]), axis=0)


def _fox_kernel(q_ref, k_ref, vt_ref, o_ref, s_scr):
    i = pl.program_id(1)
    tq = q_ref.shape[0]
    n_chunks = i // (B_CHUNK // B_TILE) + 1
    key_row = lax.broadcasted_iota(I32, (B_CHUNK, tq), 0)
    query_col = lax.broadcasted_iota(I32, (B_CHUNK, tq), 1)
    chunk_rows = lambda c: pl.ds(pl.multiple_of(c * B_CHUNK, B_CHUNK), B_CHUNK)
    lanes_of = lambda h: slice(h * LANES, (h + 1) * LANES)
    cols_of = lambda h: slice(h * tq, (h + 1) * tq)

    def scores(c, mx, causal):
        rows = chunk_rows(c)
        new = []
        for h in range(HEADS):
            s = _dot_nt(k_ref[rows, lanes_of(h)], q_ref[:, lanes_of(h)])
            if causal:
                s = jnp.where(c * B_CHUNK + key_row <= i * tq + query_col, s, NEG)
            s_scr[rows, cols_of(h)] = s
            new.append(jnp.maximum(mx[h], _fold_rows(s, jnp.max)))
        return tuple(new)

    mx = tuple(jnp.full((SUBLANES, tq), NEG, F32) for _ in range(HEADS))
    mx = lax.fori_loop(0, n_chunks - 1, lambda c, mx: scores(c, mx, False), mx)
    mx = scores(n_chunks - 1, mx, True)
    m = [jnp.max(mx[h], axis=0, keepdims=True) for h in range(HEADS)]

    def attend(c, carry):
        rows = chunk_rows(c)
        v_t = vt_ref[0, c]
        new = []
        for h in range(HEADS):
            l, acc = carry[h]
            p = jnp.exp(s_scr[rows, cols_of(h)] - m[h])
            new.append((l + _fold_rows(p, jnp.sum),
                        acc + _dot(v_t[h * HEAD_DIM:(h + 1) * HEAD_DIM, :], p.astype(BF16))))
        return tuple(new)

    init = tuple((jnp.zeros((SUBLANES, tq), F32), jnp.zeros((HEAD_DIM, tq), F32))
                 for _ in range(HEADS))
    final = lax.fori_loop(0, n_chunks, attend, init)
    out_t = jnp.concatenate([acc / jnp.sum(l, axis=0, keepdims=True) for l, acc in final],
                            axis=0)
    o_ref[...] = out_t.T.astype(o_ref.dtype)


def _fox_attention(q_aug, k_aug, v_t, batch, seq):
    t = q_aug.shape[0]
    nq = seq // B_TILE
    qw = HEADS * LANES
    return pl.pallas_call(
        _fox_kernel,
        grid=(batch, nq),
        in_specs=[pl.BlockSpec((B_TILE, qw), lambda b, i: (b * nq + i, 0)),
                  pl.BlockSpec((seq, qw), lambda b, i: (b, 0)),
                  pl.BlockSpec((1, seq // B_CHUNK, MIX_W, B_CHUNK), lambda b, i: (b, 0, 0, 0))],
        out_specs=pl.BlockSpec((B_TILE, MIX_W), lambda b, i: (b * nq + i, 0)),
        out_shape=jax.ShapeDtypeStruct((t, MIX_W), BF16),
        scratch_shapes=[pltpu.VMEM((seq, HEADS * B_TILE), F32)],
        compiler_params=_params(2),
        name="fox_attention",
    )(q_aug, k_aug, v_t)


def _fox_operands(ob, cum_planes, batch, seq):
    t = batch * seq
    planes = cum_planes.reshape(3, batch, HEADS, seq).transpose(1, 3, 2, 0).reshape(t, HEADS, 3)
    ones = jnp.ones_like(planes)
    fill = jnp.zeros((t, HEADS, LANES - HEAD_DIM - 6), BF16)
    q = ob[:, 0:MIX_W].reshape(t, HEADS, HEAD_DIM)
    k = ob[:, MIX_W:2 * MIX_W].reshape(t, HEADS, HEAD_DIM)
    q_aug = jnp.concatenate([q, planes, ones, fill], axis=-1).reshape(t, HEADS * LANES)
    k_aug = jnp.concatenate([k, ones, -planes, fill], axis=-1).reshape(t, HEADS * LANES)
    v_t = ob[:, 2 * MIX_W:3 * MIX_W].reshape(batch, seq // B_CHUNK, B_CHUNK, MIX_W)
    return q_aug, k_aug, v_t.transpose(0, 1, 3, 2)


def _sparse_kernel(kk_ref, wi_ref, wc_ref, w_ref, vt_ref, o_ref,
                   key_scr, hi_scr, lo_scr, s_scr, mx_scr, *, topk):
    i = pl.program_id(1)
    kb = C_TILE
    blocks = C_CHUNK // kb
    n_chunks = i // blocks + 1
    row = lax.broadcasted_iota(I32, (kb, LANES), 0)
    lane = lax.broadcasted_iota(I32, (kb, LANES), 1)
    limit = i * C_TILE + CHUNK * (lane // CHUNK + 1)
    w = w_ref[0, 0]
    block_rows = lambda c, u: pl.ds(pl.multiple_of(c * C_CHUNK + u * kb, kb), kb)
    chunk_rows = lambda c: pl.ds(pl.multiple_of(c * C_CHUNK, C_CHUNK), C_CHUNK)
    head = lambda h: slice(h * LANES, (h + 1) * LANES)

    def score_chunk(c, carry):
        for u in range(blocks):
            rows = block_rows(c, u)
            logits = _dot(kk_ref[rows, :], wi_ref[0, 0])
            score = functools.reduce(jnp.add, [
                w[h:h + 1, :] * jnp.maximum(logits[:, head(h)], 0.0) for h in range(HEADS)])
            bits = pltpu.bitcast(score + 0.0, I32)
            key = bits ^ ((bits >> 31) & 0x7FFFFFFF)
            key = jnp.where(c * C_CHUNK + u * kb + row < limit, key, KEY_NEG)
            key_scr[rows, :] = key
            hi_scr[rows, :] = (key >> 16).astype(I16)
            lo_scr[rows, :] = ((key & 0xFFFF) + INT16_MIN).astype(I16)
        return carry

    lax.fori_loop(0, n_chunks, score_chunk, 0)

    def count16(scr, pred):
        def body(c, acc):
            vals = scr[chunk_rows(c), :]
            parts = [jnp.where(pred(vals[g * PACKED_ROWS:(g + 1) * PACKED_ROWS, :]),
                               jnp.int16(1), jnp.int16(0))
                     for g in range(C_CHUNK // PACKED_ROWS)]
            while len(parts) > 1:
                parts = [a + b for a, b in zip(parts[0::2], parts[1::2])]
            return acc + parts[0]
        acc = lax.fori_loop(0, n_chunks, body, jnp.zeros((PACKED_ROWS, LANES), I16))
        return jnp.sum(acc.astype(I32), axis=0, keepdims=True)

    def search16(scr, need):
        def bit_step(it, thr):
            cand = thr + lax.shift_left(jnp.int32(1), 15 - it)
            cand16 = cand.astype(I16)
            return jnp.where(count16(scr, lambda v: v >= cand16) >= need, cand, thr)
        return lax.fori_loop(0, 16, bit_step, jnp.full((1, LANES), INT16_MIN, I32))

    t_hi = search16(hi_scr, topk)
    t_hi16 = t_hi.astype(I16)
    room_hi = topk - count16(hi_scr, lambda v: v > t_hi16)

    def bucket_chunk(c, carry):
        rows = chunk_rows(c)
        lo_scr[rows, :] = jnp.where(hi_scr[rows, :] == t_hi16, lo_scr[rows, :],
                                    jnp.int16(INT16_MIN))
        return carry

    lax.fori_loop(0, n_chunks, bucket_chunk, 0)
    t_lo = search16(lo_scr, room_hi)
    thr = t_hi * 65536 + (t_lo - INT16_MIN)

    def masked_scores(c, u, sel, mx):
        rows = block_rows(c, u)
        s_all = _dot(kk_ref[rows, :], wc_ref[0, 0])
        for h in range(HEADS):
            s = jnp.where(sel, s_all[:, head(h)], NEG)
            s_scr[rows, head(h)] = s
            mx[h] = jnp.maximum(mx[h], _fold_rows(s, jnp.max))

    def store_maxima(mx):
        for h in range(HEADS):
            mx_scr[h] = mx[h]

    neg_rows = lambda: jnp.full((SUBLANES, LANES), NEG, F32)

    def select_chunk(c, carry):
        mx, cnt = list(carry[:HEADS]), carry[HEADS]
        for u in range(blocks):
            key = key_scr[block_rows(c, u), :]
            sel = (c * C_CHUNK + u * kb + row < limit) & (key >= thr)
            cnt = cnt + _fold_rows(sel.astype(I32), jnp.sum)
            masked_scores(c, u, sel, mx)
        return (*mx, cnt)

    done = lax.fori_loop(0, n_chunks, select_chunk,
                         (*[neg_rows() for _ in range(HEADS)], jnp.zeros((SUBLANES, LANES), I32)))
    store_maxima(done[:HEADS])
    over = jnp.max(jnp.sum(done[HEADS], axis=0, keepdims=True)) > topk

    @pl.when(over)
    def _():
        def count_above(c, acc):
            for u in range(blocks):
                acc = acc + _fold_rows((key_scr[block_rows(c, u), :] > thr).astype(I32), jnp.sum)
            return acc
        above = lax.fori_loop(0, n_chunks, count_above, jnp.zeros((SUBLANES, LANES), I32))
        room = (topk - jnp.sum(above, axis=0, keepdims=True)).astype(F32)
        strict_lower = (lax.broadcasted_iota(I32, (kb, kb), 1)
                        < lax.broadcasted_iota(I32, (kb, kb), 0)).astype(BF16)

        def tie_chunk(c, carry):
            mx, ties_before = list(carry[:HEADS]), carry[HEADS]
            for u in range(blocks):
                key = key_scr[block_rows(c, u), :]
                tie = jnp.where(key == thr, 1.0, 0.0)
                rank = _dot(strict_lower, tie.astype(BF16)) + ties_before
                sel = ((c * C_CHUNK + u * kb + row < limit)
                       & ((key > thr) | ((key == thr) & (rank < room))))
                ties_before = ties_before + jnp.sum(tie, axis=0, keepdims=True)
                masked_scores(c, u, sel, mx)
            return (*mx, ties_before)

        redone = lax.fori_loop(0, n_chunks, tie_chunk,
                               (*[neg_rows() for _ in range(HEADS)], jnp.zeros((1, LANES), F32)))
        store_maxima(redone[:HEADS])

    m = [jnp.max(mx_scr[h], axis=0, keepdims=True) for h in range(HEADS)]

    def attend_chunk(c, carry):
        rows = chunk_rows(c)
        v_t = vt_ref[0, c]
        new = []
        for h in range(HEADS):
            l, acc = carry[h]
            p = jnp.exp(s_scr[rows, head(h)] - m[h])
            new.append((l + _fold_rows(p, jnp.sum), acc + _dot(v_t, p.astype(BF16))))
        return tuple(new)

    init = tuple((jnp.zeros((SUBLANES, LANES), F32), jnp.zeros((HEAD_DIM, LANES), F32))
                 for _ in range(HEADS))
    final = lax.fori_loop(0, n_chunks, attend_chunk, init)
    out_t = jnp.concatenate([acc / jnp.sum(l, axis=0, keepdims=True) for l, acc in final],
                            axis=0)
    o_ref[...] = out_t.T.astype(o_ref.dtype)


def _sparse_attention(oc, wi_t, wc_t, w_rows, v_t, batch, seq, topk):
    t = oc.shape[0]
    nq = seq // C_TILE
    blk4 = lambda b, i: (b, i, 0, 0)
    return pl.pallas_call(
        functools.partial(_sparse_kernel, topk=topk),
        grid=(batch, nq),
        in_specs=[pl.BlockSpec((seq, LANES), lambda b, i: (b, (_KK - _QC) // LANES)),
                  pl.BlockSpec((1, 1, LANES, HEADS * C_TILE), blk4),
                  pl.BlockSpec((1, 1, LANES, HEADS * C_TILE), blk4),
                  pl.BlockSpec((1, 1, HEADS, C_TILE), blk4),
                  pl.BlockSpec((1, seq // C_CHUNK, HEAD_DIM, C_CHUNK), lambda b, i: (b, 0, 0, 0))],
        out_specs=pl.BlockSpec((C_TILE, MIX_W), lambda b, i: (b * nq + i, 0)),
        out_shape=jax.ShapeDtypeStruct((t, MIX_W), BF16),
        scratch_shapes=[pltpu.VMEM((seq, LANES), I32),
                        pltpu.VMEM((seq, LANES), I16),
                        pltpu.VMEM((seq, LANES), I16),
                        pltpu.VMEM((seq, HEADS * LANES), F32),
                        pltpu.VMEM((HEADS, SUBLANES, LANES), F32)],
        compiler_params=_params(2),
        name="sparse_attention",
    )(oc, wi_t, wc_t, w_rows, v_t)


def _sparse_operands(oc, small, batch, seq):
    nq = seq // C_TILE

    def transposed(cols):
        q = cols.reshape(batch, nq, C_TILE, HEADS, HEAD_DIM)
        return q.transpose(0, 1, 4, 3, 2).reshape(batch, nq, HEAD_DIM, HEADS * C_TILE)

    zeros = jnp.zeros((batch, nq, HEAD_DIM, HEADS * C_TILE), BF16)
    wc_t = jnp.concatenate([transposed(oc[:, 0:MIX_W]), zeros], axis=2)
    wi_t = jnp.concatenate([zeros, transposed(oc[:, MIX_W:2 * MIX_W])], axis=2)
    w_rows = small[:, HEADS:2 * HEADS].reshape(batch, nq, C_TILE, HEADS).transpose(0, 1, 3, 2)
    v = oc[:, _VC - _QC:_VC - _QC + HEAD_DIM]
    v_t = v.reshape(batch, seq // C_CHUNK, C_CHUNK, HEAD_DIM).transpose(0, 1, 3, 2)
    return wi_t, wc_t, w_rows, v_t


def _rms_scale(y, g):
    return y * lax.rsqrt(jnp.mean(y * y, axis=-1, keepdims=True) + EPS) * g


def _merge_kernel(ya_ref, yb_ref, yc_ref, gt_ref, x_ref, wb_ref, wo_ref, g_ref, gate_ref, o_ref):
    d = x_ref.shape[1]
    merged = functools.reduce(jnp.add, [
        gt_ref[:, n * d:(n + 1) * d].astype(F32) * _dot(y_ref[...], wb_ref[n])
        for n, y_ref in enumerate((ya_ref, yb_ref, yc_ref))])
    y = _dot(merged.astype(BF16), wo_ref[...])
    o_ref[...] = x_ref[...] + gate_ref[0] * _rms_scale(y, g_ref[...])


def _merge_projection(ya, yb, yc, gates, x2, w_branch, w_out, g, gate, seq):
    t, d = x2.shape
    tm = min(ROW_TILE, seq)
    per_seq = seq // tm
    row = lambda i: (i, 0)
    return pl.pallas_call(
        _merge_kernel,
        grid=(t // tm,),
        in_specs=[pl.BlockSpec((tm, MIX_W), row)] * 3
        + [pl.BlockSpec((tm, N_BRANCH * d), row),
           pl.BlockSpec((tm, d), row),
           _resident((N_BRANCH, MIX_W, d), lambda i: (0, 0, 0)),
           _resident((d, d), lambda i: (0, 0)),
           pl.BlockSpec((1, d), lambda i: (0, 0)),
           pl.BlockSpec((1, 1, d), lambda i: (i // per_seq, 0, 0))],
        out_specs=pl.BlockSpec((tm, d), row),
        out_shape=jax.ShapeDtypeStruct((t, d), F32),
        compiler_params=_params(1),
        name="merge_projection",
    )(ya, yb, yc, gates, x2, w_branch, w_out, g, gate)


def _ffn_kernel(x_ref, g_ref, sc_ref, sh_ref, wu_ref, cw_ref, cb_ref, wd_ref, g2_ref, gate_ref,
                o_ref, a_scr, tail_scr, *, per_seq):
    i = pl.program_id(0)
    tm = x_ref.shape[0]
    halo = SUBLANES
    x = x_ref[...]
    hb = _modulated_norm(x, g_ref[...], sc_ref[0], sh_ref[0]).astype(BF16)
    seq_start = (i % per_seq) == 0
    acc = jnp.zeros(x.shape, F32)
    for c in range(D_FF // FF_TILE):
        cs = slice(c * FF_TILE, (c + 1) * FF_TILE)
        a = _dot(hb, wu_ref[:, cs])
        gate = _dot(hb, wu_ref[:, D_FF + c * FF_TILE:D_FF + (c + 1) * FF_TILE])
        a_scr[0:halo, :] = jnp.where(seq_start, 0.0, tail_scr[:, cs])
        a_scr[halo:halo + tm, :] = a
        tail_scr[:, cs] = a[tm - halo:tm, :]
        conv = (cw_ref[0:1, cs] * a_scr[halo - 2:halo - 2 + tm, :]
                + cw_ref[1:2, cs] * a_scr[halo - 1:halo - 1 + tm, :]
                + cw_ref[2:3, cs] * a + cb_ref[:, cs])
        acc = acc + _dot((jax.nn.gelu(conv, approximate=True) * gate).astype(BF16), wd_ref[cs, :])
    o_ref[...] = x + gate_ref[0] * _rms_scale(acc, g2_ref[...])


def _conv_ffn(x2, g, sc, sh, w_up, conv_w, conv_b, w_down, g2, gate, seq):
    t, d = x2.shape
    tm = min(ROW_TILE, seq)
    per_seq = seq // tm
    row = lambda i: (i, 0)
    const = lambda i: (0, 0)
    by_batch = lambda i: (i // per_seq, 0, 0)
    return pl.pallas_call(
        functools.partial(_ffn_kernel, per_seq=per_seq),
        grid=(t // tm,),
        in_specs=[pl.BlockSpec((tm, d), row),
                  pl.BlockSpec((1, d), const),
                  pl.BlockSpec((1, 1, d), by_batch),
                  pl.BlockSpec((1, 1, d), by_batch),
                  _resident((d, 2 * D_FF), const),
                  pl.BlockSpec((CONV_W, D_FF), const),
                  pl.BlockSpec((1, D_FF), const),
                  _resident((D_FF, d), const),
                  pl.BlockSpec((1, d), const),
                  pl.BlockSpec((1, 1, d), by_batch)],
        out_specs=pl.BlockSpec((tm, d), row),
        out_shape=jax.ShapeDtypeStruct((t, d), F32),
        scratch_shapes=[pltpu.VMEM((tm + SUBLANES, FF_TILE), F32),
                        pltpu.VMEM((SUBLANES, D_FF), F32)],
        compiler_params=_params(1),
        name="conv_ffn",
    )(x2, g, sc, sh, w_up, conv_w, conv_b, w_down, g2, gate)


def kernel(x, c, w_ada, b_ada, norm_g, w_in, b_gate, rel_table, b_forget, w_branch, w_out,
           w_up, conv_w, conv_b, w_down):
    batch, seq, d = x.shape
    depth = w_ada.shape[0]
    t = batch * seq
    topk = min(TOPK_MAX, seq // 4)
    cos_t, sin_t = _rope_tables(seq)
    mod = _modulation(c, w_ada, b_ada)
    x2 = x.reshape(t, d)
    for l in range(depth):
        sh1, sc1, g1, sh2, sc2, g2 = (mod[l, :, n * d:(n + 1) * d].reshape(batch, 1, d)
                                      for n in range(6))
        oa, ob, oc, small, gates = _in_projection(
            x2, norm_g[l, 0:1], sc1, sh1, _layout_w_in(w_in[l]),
            b_gate[l].reshape(1, N_BRANCH * d), cos_t, sin_t, seq)

        ya = _band_attention(oa, _band_bias(rel_table[l]), batch, seq)

        f_rows = small[:, 0:HEADS].reshape(batch, seq, HEADS).transpose(0, 2, 1)
        planes = _forget_prefix(f_rows.reshape(batch * HEADS, seq),
                                jnp.tile(b_forget[l], batch).reshape(batch * HEADS, 1))
        yb = _fox_attention(*_fox_operands(ob, planes, batch, seq), batch, seq)

        yc = _sparse_attention(oc, *_sparse_operands(oc, small, batch, seq), batch, seq, topk)

        x2 = _merge_projection(ya, yb, yc, gates, x2, w_branch[l].astype(BF16),
                               w_out[l].astype(BF16), norm_g[l, 1:2], g1, seq)
        x2 = _conv_ffn(x2, norm_g[l, 2:3], sc2, sh2, w_up[l].astype(BF16), conv_w[l],
                       conv_b[l].reshape(1, D_FF), w_down[l].astype(BF16), norm_g[l, 3:4], g2, seq)
    return x2.reshape(batch, seq, d)
```

```python
import functools

import numpy as np
import jax
import jax.numpy as jnp
from jax import lax
from jax.experimental import pallas as pl
from jax.experimental.pallas import tpu as pltpu

F32 = jnp.float32
BF16 = jnp.bfloat16
I32 = jnp.int32

D_MODEL = 1024
CHUNK = 64
HEAD_DIM = 64
HEADS = 4
A_LEFT_CHUNKS = 8
A_MAX_REL = 128
IDX_DIM = 64
TOPK_MAX = 256
N_BRANCH = 3
MIX_W = 256
D_FF = 2816
CONV_W = 3
ROPE_THETA = 10000.0
EPS = 1e-6
NEG = -1e30

LANES = 128
SUBLANES = 8
VMEM_LIMIT = 56 * 1024 * 1024

_QA, _KA, _VA = 0, 512, 1024
_QB, _KB, _VB = 1280, 1536, 1792
_QC, _QI, _KK, _VC = 2048, 2304, 2560, 2688
_SM, _GT, _N_COLS = 2816, 2944, 6016

ROW_TILE = 512
A_TILE = 256
B_TILE = 256
B_CHUNK = 512
C_TILE = 256
C_KEYS = 128
C_CHUNK = 512
FF_TILE = 256

PACKED_ROWS = 16
INT16_MIN = -2 ** 15
I16 = jnp.int16


def _sortable_key_of(value):
    bits = int(np.array(value, np.float32).view(np.int32))
    return bits ^ ((bits >> 31) & 0x7FFFFFFF)


KEY_NEG = _sortable_key_of(NEG)


def _dot(a, b):
    return jnp.dot(a, b, preferred_element_type=F32)


def _dot_nt(a, b):
    return lax.dot_general(a, b, (((1,), (1,)), ((), ())), preferred_element_type=F32)


def _params(n_axes):
    return pltpu.CompilerParams(dimension_semantics=("arbitrary",) * n_axes,
                                vmem_limit_bytes=VMEM_LIMIT)


def _resident(shape, index_map):
    return pl.BlockSpec(shape, index_map, pipeline_mode=pl.Buffered(1))


def _mod_kernel(c_ref, w_ref, b_ref, o_ref):
    c = c_ref[...]
    c_act = c * jax.nn.sigmoid(c)
    o_ref[0] = _dot(c_act.astype(BF16), w_ref[0].astype(BF16)) + b_ref[0]


def _modulation(c, w_ada, b_ada):
    depth, d, n = w_ada.shape
    b = c.shape[0]
    return pl.pallas_call(
        _mod_kernel,
        grid=(depth, n // d),
        in_specs=[pl.BlockSpec((b, d), lambda l, j: (0, 0)),
                  pl.BlockSpec((1, d, d), lambda l, j: (l, 0, j)),
                  pl.BlockSpec((1, 1, d), lambda l, j: (l, 0, j))],
        out_specs=pl.BlockSpec((1, b, d), lambda l, j: (l, 0, j)),
        out_shape=jax.ShapeDtypeStruct((depth, b, n), F32),
        compiler_params=_params(2),
        name="adaln_mod",
    )(c, w_ada, b_ada.reshape(depth, 1, n))


def _modulated_norm(x, g, sc, sh):
    ms = jnp.mean(x * x, axis=-1, keepdims=True)
    return (x * lax.rsqrt(ms + EPS) * g) * (1.0 + sc) + sh


def _in_kernel(x_ref, g_ref, sc_ref, sh_ref, w_ref, bg_ref, cos_ref, sin_ref,
               oa_ref, ob_ref, oc_ref, os_ref, og_ref):
    hb = _modulated_norm(x_ref[...], g_ref[...], sc_ref[0], sh_ref[0]).astype(BF16)
    proj = lambda lo, hi: _dot(hb, w_ref[:, lo:hi])
    oa_ref[...] = proj(_QA, _QB).astype(BF16)
    ob_ref[...] = proj(_QB, _QC).astype(BF16)
    zr = proj(_QC, _VC)
    lane = lax.broadcasted_iota(I32, (zr.shape[0], LANES), 1)
    first_half = (lane % HEAD_DIM) < (HEAD_DIM // 2)
    cos = cos_ref[...]
    sin = sin_ref[...]
    for s in range((_VC - _QC) // LANES):
        xs = zr[:, s * LANES:(s + 1) * LANES]
        swapped = jnp.where(first_half, pltpu.roll(xs, LANES - HEAD_DIM // 2, 1),
                            pltpu.roll(xs, HEAD_DIM // 2, 1))
        oc_ref[:, s * LANES:(s + 1) * LANES] = (xs * cos + swapped * sin).astype(BF16)
    oc_ref[:, _VC - _QC:_SM - _QC] = proj(_VC, _SM).astype(BF16)
    os_ref[...] = proj(_SM, _GT)
    og_ref[...] = jax.nn.sigmoid(proj(_GT, _N_COLS) + bg_ref[...]).astype(BF16)


def _in_projection(x2, g, sc, sh, w, b_gate, cos_t, sin_t, seq):
    t, d = x2.shape
    tm = min(ROW_TILE, seq)
    per_seq = seq // tm
    row = lambda i: (i, 0)
    const = lambda i: (0, 0)
    by_batch = lambda i: (i // per_seq, 0, 0)
    by_pos = lambda i: (i % per_seq, 0)
    widths = (_QB - _QA, _QC - _QB, _SM - _QC, _GT - _SM, _N_COLS - _GT)
    dtypes = (BF16, BF16, BF16, F32, BF16)
    return pl.pallas_call(
        _in_kernel,
        grid=(t // tm,),
        in_specs=[pl.BlockSpec((tm, d), row),
                  pl.BlockSpec((1, d), const),
                  pl.BlockSpec((1, 1, d), by_batch),
                  pl.BlockSpec((1, 1, d), by_batch),
                  _resident((d, _N_COLS), const),
                  pl.BlockSpec((1, _N_COLS - _GT), const),
                  pl.BlockSpec((tm, LANES), by_pos),
                  pl.BlockSpec((tm, LANES), by_pos)],
        out_specs=[pl.BlockSpec((tm, wd), row) for wd in widths],
        out_shape=[jax.ShapeDtypeStruct((t, wd), dt) for wd, dt in zip(widths, dtypes)],
        compiler_params=_params(1),
        name="in_projection",
    )(x2, g, sc, sh, w, b_gate, cos_t, sin_t)


def _layout_w_in(w_in):
    sizes = (MIX_W,) * 6 + (HEADS, MIX_W, HEAD_DIM, HEAD_DIM, HEADS * IDX_DIM, IDX_DIM, HEADS,
                            N_BRANCH * D_MODEL)
    offs = np.cumsum((0,) + sizes)
    qa, ka, va, qb, kb, vb, fb, qc, kc, vc, qi, ki, wi, zg = (
        w_in[:, offs[n]:offs[n + 1]] for n in range(len(sizes)))
    d = w_in.shape[0]
    scale = HEAD_DIM ** -0.5

    def pad_heads(w):
        w = w.reshape(d, HEADS, HEAD_DIM)
        return jnp.pad(w, ((0, 0), (0, 0), (0, LANES - HEAD_DIM))).reshape(d, HEADS * LANES)

    zeros = lambda n: jnp.zeros((d, n), w_in.dtype)
    cols = [pad_heads(qa * scale), pad_heads(ka), va,
            qb * scale, kb, vb,
            qc * scale, qi * (IDX_DIM ** -0.5), kc, ki,
            vc, zeros(LANES - HEAD_DIM),
            fb, wi * (HEADS ** -0.5), zeros(LANES - 2 * HEADS),
            zg]
    w = jnp.concatenate(cols, axis=1)
    assert w.shape[1] == _N_COLS
    return w.astype(BF16)


def _rope_tables(seq):
    pos = jnp.arange(seq, dtype=F32)
    inv_freq = ROPE_THETA ** (-jnp.arange(0, HEAD_DIM, 2, dtype=F32) / HEAD_DIM)
    ang = pos[:, None] * inv_freq[None, :]
    cos, sin = jnp.cos(ang), jnp.sin(ang)
    reps = LANES // HEAD_DIM
    cos_t = jnp.tile(jnp.concatenate([cos, cos], axis=1), (1, reps))
    sin_t = jnp.tile(jnp.concatenate([-sin, sin], axis=1), (1, reps))
    return cos_t, sin_t


def _cum_kernel(f_ref, b_ref, o_ref):
    z = f_ref[...] + b_ref[...]
    c = jnp.minimum(z, 0.0) - jnp.log1p(jnp.exp(-jnp.abs(z)))
    n = c.shape[1]
    lane = lax.broadcasted_iota(I32, c.shape, 1)
    shift = 1
    while shift < n:
        c = c + jnp.where(lane >= shift, pltpu.roll(c, shift, 1), 0.0)
        shift *= 2
    hi = c.astype(BF16)
    r1 = c - hi.astype(F32)
    mid = r1.astype(BF16)
    lo = (r1 - mid.astype(F32)).astype(BF16)
    o_ref[0] = hi
    o_ref[1] = mid
    o_ref[2] = lo


def _forget_prefix(f_rows, b_col):
    r, s = f_rows.shape
    return pl.pallas_call(
        _cum_kernel,
        grid=(1,),
        in_specs=[pl.BlockSpec((r, s), lambda i: (0, 0)),
                  pl.BlockSpec((r, 1), lambda i: (0, 0))],
        out_specs=pl.BlockSpec((3, r, s), lambda i: (0, 0, 0)),
        out_shape=jax.ShapeDtypeStruct((3, r, s), BF16),
        compiler_params=_params(1),
        name="forget_prefix",
    )(f_rows, b_col)


def _merge_head_pairs(o_ref, outs):
    lane = lax.broadcasted_iota(I32, outs[0].shape, 1)
    for p in range(HEADS // 2):
        o_ref[:, p * LANES:(p + 1) * LANES] = jnp.where(
            lane < HEAD_DIM, outs[2 * p], outs[2 * p + 1]).astype(o_ref.dtype)


def _band_kernel(q_ref, k0_ref, k1_ref, k2_ref, v0_ref, v1_ref, v2_ref, bias_ref, o_ref):
    i = pl.program_id(1)
    k_refs = (k0_ref, k1_ref, k2_ref)
    v_refs = (v0_ref, v1_ref, v2_ref)
    n_kb = len(k_refs)
    outs = []
    for h in range(HEADS):
        hs = slice(h * LANES, (h + 1) * LANES)
        vs = slice((h // 2) * LANES, (h // 2 + 1) * LANES)
        q = q_ref[:, hs]
        s = []
        for j in range(n_kb):
            sj = _dot_nt(q, k_refs[j][:, hs]) + bias_ref[h, :, j * A_TILE:(j + 1) * A_TILE]
            if j < n_kb - 1:
                sj = jnp.where(i - (n_kb - 1) + j >= 0, sj, NEG)
            s.append(sj)
        m = functools.reduce(jnp.maximum, [sj.max(axis=1, keepdims=True) for sj in s])
        p = [jnp.exp(sj - m) for sj in s]
        l = functools.reduce(jnp.add, [pj.sum(axis=1, keepdims=True) for pj in p])
        acc = functools.reduce(jnp.add, [_dot(p[j].astype(BF16), v_refs[j][:, vs])
                                         for j in range(n_kb)])
        outs.append(acc / l)
    _merge_head_pairs(o_ref, outs)


def _band_bias(rel_table):
    n_kb = A_LEFT_CHUNKS * CHUNK // A_TILE + 1
    width = n_kb * A_TILE
    heads = rel_table.shape[0]
    span = A_TILE + width - 1
    top = (n_kb - 1) * A_TILE + A_TILE - 1
    n_hi = top - A_MAX_REL
    n_lo = span - n_hi - (2 * A_MAX_REL + 1)
    f = jnp.concatenate([jnp.repeat(rel_table[:, :1], n_lo, axis=1), rel_table,
                         jnp.repeat(rel_table[:, -1:], n_hi, axis=1)], axis=1)
    g = jnp.pad(f[:, ::-1], ((0, 0), (0, 1)))
    period = span + 1
    skew = jnp.tile(g, (1, A_TILE + 1))[:, :A_TILE * (period + 1)]
    skew = skew.reshape(heads, A_TILE, period + 1)[:, :, :width]
    bias = skew[:, ::-1, :].astype(F32)
    qi = np.arange(A_TILE)[:, None]
    kj = np.arange(width)[None, :]
    dc = qi // CHUNK + (n_kb - 1) * A_TILE // CHUNK - kj // CHUNK
    band = (dc >= 0) & (dc <= A_LEFT_CHUNKS)
    return jnp.where(jnp.asarray(band)[None], bias, NEG)


def _band_attention(oa, bias, batch, seq):
    t = oa.shape[0]
    nq = seq // A_TILE
    n_kb = bias.shape[2] // A_TILE
    qw = HEADS * LANES

    def k_spec(j, width, col):
        return pl.BlockSpec(
            (A_TILE, width),
            lambda b, i: (b * nq + jnp.maximum(i - (n_kb - 1) + j, 0), col))

    return pl.pallas_call(
        _band_kernel,
        grid=(batch, nq),
        in_specs=([pl.BlockSpec((A_TILE, qw), lambda b, i: (b * nq + i, _QA // qw))]
                  + [k_spec(j, qw, _KA // qw) for j in range(n_kb)]
                  + [k_spec(j, MIX_W, _VA // MIX_W) for j in range(n_kb)]
                  + [pl.BlockSpec(bias.shape, lambda b, i: (0, 0, 0))]),
        out_specs=pl.BlockSpec((A_TILE, MIX_W), lambda b, i: (b * nq + i, 0)),
        out_shape=jax.ShapeDtypeStruct((t, MIX_W), BF16),
        compiler_params=_params(2),
        name="band_attention",
    )(oa, *([oa] * (2 * n_kb)), bias)


def _fold_rows(x, op):
    return op(x.reshape(x.shape[0] // SUBLANES, SUBLANES, x.shape[1]), axis=0)


def _fox_kernel(q_ref, k_ref, vt_ref, o_ref, s_scr):
    i = pl.program_id(1)
    tq = q_ref.shape[0]
    n_chunks = i // (B_CHUNK // B_TILE) + 1
    key_row = lax.broadcasted_iota(I32, (B_CHUNK, tq), 0)
    query_col = lax.broadcasted_iota(I32, (B_CHUNK, tq), 1)
    chunk_rows = lambda c: pl.ds(pl.multiple_of(c * B_CHUNK, B_CHUNK), B_CHUNK)
    lanes_of = lambda h: slice(h * LANES, (h + 1) * LANES)
    cols_of = lambda h: slice(h * tq, (h + 1) * tq)

    def scores(c, mx, causal):
        rows = chunk_rows(c)
        new = []
        for h in range(HEADS):
            s = _dot_nt(k_ref[rows, lanes_of(h)], q_ref[:, lanes_of(h)])
            if causal:
                s = jnp.where(c * B_CHUNK + key_row <= i * tq + query_col, s, NEG)
            s_scr[rows, cols_of(h)] = s
            new.append(jnp.maximum(mx[h], _fold_rows(s, jnp.max)))
        return tuple(new)

    mx = tuple(jnp.full((SUBLANES, tq), NEG, F32) for _ in range(HEADS))
    mx = lax.fori_loop(0, n_chunks - 1, lambda c, mx: scores(c, mx, False), mx)
    mx = scores(n_chunks - 1, mx, True)
    m = [jnp.max(mx[h], axis=0, keepdims=True) for h in range(HEADS)]

    def attend(c, carry):
        rows = chunk_rows(c)
        v_t = vt_ref[0, c]
        new = []
        for h in range(HEADS):
            l, acc = carry[h]
            p = jnp.exp(s_scr[rows, cols_of(h)] - m[h])
            new.append((l + _fold_rows(p, jnp.sum),
                        acc + _dot(v_t[h * HEAD_DIM:(h + 1) * HEAD_DIM, :], p.astype(BF16))))
        return tuple(new)

    init = tuple((jnp.zeros((SUBLANES, tq), F32), jnp.zeros((HEAD_DIM, tq), F32))
                 for _ in range(HEADS))
    final = lax.fori_loop(0, n_chunks, attend, init)
    out_t = jnp.concatenate([acc / jnp.sum(l, axis=0, keepdims=True) for l, acc in final],
                            axis=0)
    o_ref[...] = out_t.T.astype(o_ref.dtype)


def _fox_attention(q_aug, k_aug, v_t, batch, seq):
    t = q_aug.shape[0]
    nq = seq // B_TILE
    qw = HEADS * LANES
    return pl.pallas_call(
        _fox_kernel,
        grid=(batch, nq),
        in_specs=[pl.BlockSpec((B_TILE, qw), lambda b, i: (b * nq + i, 0)),
                  pl.BlockSpec((seq, qw), lambda b, i: (b, 0)),
                  pl.BlockSpec((1, seq // B_CHUNK, MIX_W, B_CHUNK), lambda b, i: (b, 0, 0, 0))],
        out_specs=pl.BlockSpec((B_TILE, MIX_W), lambda b, i: (b * nq + i, 0)),
        out_shape=jax.ShapeDtypeStruct((t, MIX_W), BF16),
        scratch_shapes=[pltpu.VMEM((seq, HEADS * B_TILE), F32)],
        compiler_params=_params(2),
        name="fox_attention",
    )(q_aug, k_aug, v_t)


def _fox_operands(ob, cum_planes, batch, seq):
    t = batch * seq
    planes = cum_planes.reshape(3, batch, HEADS, seq).transpose(1, 3, 2, 0).reshape(t, HEADS, 3)
    ones = jnp.ones_like(planes)
    fill = jnp.zeros((t, HEADS, LANES - HEAD_DIM - 6), BF16)
    q = ob[:, 0:MIX_W].reshape(t, HEADS, HEAD_DIM)
    k = ob[:, MIX_W:2 * MIX_W].reshape(t, HEADS, HEAD_DIM)
    q_aug = jnp.concatenate([q, planes, ones, fill], axis=-1).reshape(t, HEADS * LANES)
    k_aug = jnp.concatenate([k, ones, -planes, fill], axis=-1).reshape(t, HEADS * LANES)
    v_t = ob[:, 2 * MIX_W:3 * MIX_W].reshape(batch, seq // B_CHUNK, B_CHUNK, MIX_W)
    return q_aug, k_aug, v_t.transpose(0, 1, 3, 2)


def _sparse_kernel(kk_ref, wi_ref, wc_ref, w_ref, vt_ref, o_ref,
                   key_scr, hi_scr, lo_scr, s_scr, *, topk):
    i = pl.program_id(1)
    kb = C_KEYS
    nq = C_TILE
    blocks = C_CHUNK // kb
    n_chunks = (i * nq + nq - 1) // C_CHUNK + 1
    row = lax.broadcasted_iota(I32, (kb, nq), 0)
    lane = lax.broadcasted_iota(I32, (kb, nq), 1)
    limit = i * nq + CHUNK * (lane // CHUNK + 1)
    w = w_ref[0, 0]
    block_rows = lambda c, u: pl.ds(pl.multiple_of(c * C_CHUNK + u * kb, kb), kb)
    chunk_rows = lambda c: pl.ds(pl.multiple_of(c * C_CHUNK, C_CHUNK), C_CHUNK)
    head = lambda h: slice(h * nq, (h + 1) * nq)

    def score_chunk(c, carry):
        for u in range(blocks):
            rows = block_rows(c, u)
            logits = _dot(kk_ref[rows, :], wi_ref[0, 0])
            score = functools.reduce(jnp.add, [
                w[h:h + 1, :] * jnp.maximum(logits[:, head(h)], 0.0) for h in range(HEADS)])
            bits = pltpu.bitcast(score + 0.0, I32)
            key = bits ^ ((bits >> 31) & 0x7FFFFFFF)
            key = jnp.where(c * C_CHUNK + u * kb + row < limit, key, KEY_NEG)
            key_scr[rows, :] = key
            hi_scr[rows, :] = (key >> 16).astype(I16)
            lo_scr[rows, :] = ((key & 0xFFFF) + INT16_MIN).astype(I16)
        return carry

    lax.fori_loop(0, n_chunks, score_chunk, 0)

    def count16(scr, pred):
        def body(c, acc):
            vals = scr[chunk_rows(c), :]
            parts = [jnp.where(pred(vals[g * PACKED_ROWS:(g + 1) * PACKED_ROWS, :]),
                               jnp.int16(1), jnp.int16(0))
                     for g in range(C_CHUNK // PACKED_ROWS)]
            while len(parts) > 1:
                parts = [a + b for a, b in zip(parts[0::2], parts[1::2])]
            return acc + parts[0]
        acc = lax.fori_loop(0, n_chunks, body, jnp.zeros((PACKED_ROWS, nq), I16))
        return jnp.sum(acc.astype(I32), axis=0, keepdims=True)

    def search16(scr, need):
        def bit_step(it, thr):
            cand = thr + lax.shift_left(jnp.int32(1), 15 - it)
            cand16 = cand.astype(I16)
            return jnp.where(count16(scr, lambda v: v >= cand16) >= need, cand, thr)
        return lax.fori_loop(0, 16, bit_step, jnp.full((1, nq), INT16_MIN, I32))

    t_hi = search16(hi_scr, topk)
    t_hi16 = t_hi.astype(I16)
    room_hi = topk - count16(hi_scr, lambda v: v > t_hi16)

    def bucket_chunk(c, carry):
        rows = chunk_rows(c)
        lo_scr[rows, :] = jnp.where(hi_scr[rows, :] == t_hi16, lo_scr[rows, :],
                                    jnp.int16(INT16_MIN))
        return carry

    lax.fori_loop(0, n_chunks, bucket_chunk, 0)
    t_lo = search16(lo_scr, room_hi)
    thr = t_hi * 65536 + (t_lo - INT16_MIN)

    t_lo16 = t_lo.astype(I16)
    room = (room_hi - count16(lo_scr, lambda v: v > t_lo16)).astype(F32)
    strict_lower = (lax.broadcasted_iota(I32, (kb, kb), 1)
                    < lax.broadcasted_iota(I32, (kb, kb), 0)).astype(BF16)

    def select_chunk(c, carry):
        mx, ties_before = list(carry[:HEADS]), carry[HEADS]
        for u in range(blocks):
            rows = block_rows(c, u)
            key = key_scr[rows, :]
            tie = jnp.where(key == thr, 1.0, 0.0)
            rank = _dot(strict_lower, tie.astype(BF16)) + ties_before
            sel = ((c * C_CHUNK + u * kb + row < limit)
                   & ((key > thr) | ((key == thr) & (rank < room))))
            ties_before = ties_before + jnp.sum(tie, axis=0, keepdims=True)
            s_all = _dot(kk_ref[rows, :], wc_ref[0, 0])
            for h in range(HEADS):
                s = jnp.where(sel, s_all[:, head(h)], NEG)
                s_scr[rows, head(h)] = s
                mx[h] = jnp.maximum(mx[h], _fold_rows(s, jnp.max))
        return (*mx, ties_before)

    done = lax.fori_loop(0, n_chunks, select_chunk,
                         (*[jnp.full((SUBLANES, nq), NEG, F32) for _ in range(HEADS)],
                          jnp.zeros((1, nq), F32)))

    m = [jnp.max(done[h], axis=0, keepdims=True) for h in range(HEADS)]

    def attend_chunk(c, carry):
        rows = chunk_rows(c)
        v_t = vt_ref[0, c]
        new = []
        for h in range(HEADS):
            l, acc = carry[h]
            p = jnp.exp(s_scr[rows, head(h)] - m[h])
            new.append((l + _fold_rows(p, jnp.sum), acc + _dot(v_t, p.astype(BF16))))
        return tuple(new)

    init = tuple((jnp.zeros((SUBLANES, nq), F32), jnp.zeros((HEAD_DIM, nq), F32))
                 for _ in range(HEADS))
    final = lax.fori_loop(0, n_chunks, attend_chunk, init)
    out_t = jnp.concatenate([acc / jnp.sum(l, axis=0, keepdims=True) for l, acc in final],
                            axis=0)
    o_ref[...] = out_t.T.astype(o_ref.dtype)


def _sparse_attention(oc, wi_t, wc_t, w_rows, v_t, batch, seq, topk):
    t = oc.shape[0]
    nq = seq // C_TILE
    blk4 = lambda b, i: (b, i, 0, 0)
    return pl.pallas_call(
        functools.partial(_sparse_kernel, topk=topk),
        grid=(batch, nq),
        in_specs=[pl.BlockSpec((seq, LANES), lambda b, i: (b, (_KK - _QC) // LANES)),
                  pl.BlockSpec((1, 1, LANES, HEADS * C_TILE), blk4),
                  pl.BlockSpec((1, 1, LANES, HEADS * C_TILE), blk4),
                  pl.BlockSpec((1, 1, HEADS, C_TILE), blk4),
                  pl.BlockSpec((1, seq // C_CHUNK, HEAD_DIM, C_CHUNK), lambda b, i: (b, 0, 0, 0))],
        out_specs=pl.BlockSpec((C_TILE, MIX_W), lambda b, i: (b * nq + i, 0)),
        out_shape=jax.ShapeDtypeStruct((t, MIX_W), BF16),
        scratch_shapes=[pltpu.VMEM((seq, C_TILE), I32),
                        pltpu.VMEM((seq, C_TILE), I16),
                        pltpu.VMEM((seq, C_TILE), I16),
                        pltpu.VMEM((seq, HEADS * C_TILE), F32)],
        compiler_params=_params(2),
        name="sparse_attention",
    )(oc, wi_t, wc_t, w_rows, v_t)


def _sparse_operands(oc, small, batch, seq):
    nq = seq // C_TILE

    def transposed(cols):
        q = cols.reshape(batch, nq, C_TILE, HEADS, HEAD_DIM)
        return q.transpose(0, 1, 4, 3, 2).reshape(batch, nq, HEAD_DIM, HEADS * C_TILE)

    zeros = jnp.zeros((batch, nq, HEAD_DIM, HEADS * C_TILE), BF16)
    wc_t = jnp.concatenate([transposed(oc[:, 0:MIX_W]), zeros], axis=2)
    wi_t = jnp.concatenate([zeros, transposed(oc[:, MIX_W:2 * MIX_W])], axis=2)
    w_rows = small[:, HEADS:2 * HEADS].reshape(batch, nq, C_TILE, HEADS).transpose(0, 1, 3, 2)
    v = oc[:, _VC - _QC:_VC - _QC + HEAD_DIM]
    v_t = v.reshape(batch, seq // C_CHUNK, C_CHUNK, HEAD_DIM).transpose(0, 1, 3, 2)
    return wi_t, wc_t, w_rows, v_t


def _rms_scale(y, g):
    return y * lax.rsqrt(jnp.mean(y * y, axis=-1, keepdims=True) + EPS) * g


def _merge_kernel(ya_ref, yb_ref, yc_ref, gt_ref, x_ref, wb_ref, wo_ref, g_ref, gate_ref, o_ref):
    d = x_ref.shape[1]
    merged = functools.reduce(jnp.add, [
        gt_ref[:, n * d:(n + 1) * d].astype(F32) * _dot(y_ref[...], wb_ref[n])
        for n, y_ref in enumerate((ya_ref, yb_ref, yc_ref))])
    y = _dot(merged.astype(BF16), wo_ref[...])
    o_ref[...] = x_ref[...] + gate_ref[0] * _rms_scale(y, g_ref[...])


def _merge_projection(ya, yb, yc, gates, x2, w_branch, w_out, g, gate, seq):
    t, d = x2.shape
    tm = min(ROW_TILE, seq)
    per_seq = seq // tm
    row = lambda i: (i, 0)
    return pl.pallas_call(
        _merge_kernel,
        grid=(t // tm,),
        in_specs=[pl.BlockSpec((tm, MIX_W), row)] * 3
        + [pl.BlockSpec((tm, N_BRANCH * d), row),
           pl.BlockSpec((tm, d), row),
           _resident((N_BRANCH, MIX_W, d), lambda i: (0, 0, 0)),
           _resident((d, d), lambda i: (0, 0)),
           pl.BlockSpec((1, d), lambda i: (0, 0)),
           pl.BlockSpec((1, 1, d), lambda i: (i // per_seq, 0, 0))],
        out_specs=pl.BlockSpec((tm, d), row),
        out_shape=jax.ShapeDtypeStruct((t, d), F32),
        compiler_params=_params(1),
        name="merge_projection",
    )(ya, yb, yc, gates, x2, w_branch, w_out, g, gate)


def _ffn_kernel(x_ref, g_ref, sc_ref, sh_ref, wu_ref, cw_ref, cb_ref, wd_ref, g2_ref, gate_ref,
                o_ref, a_scr, tail_scr, *, per_seq):
    i = pl.program_id(0)
    tm = x_ref.shape[0]
    halo = SUBLANES
    x = x_ref[...]
    hb = _modulated_norm(x, g_ref[...], sc_ref[0], sh_ref[0]).astype(BF16)
    seq_start = (i % per_seq) == 0
    acc = jnp.zeros(x.shape, F32)
    for c in range(D_FF // FF_TILE):
        cs = slice(c * FF_TILE, (c + 1) * FF_TILE)
        a = _dot(hb, wu_ref[:, cs])
        gate = _dot(hb, wu_ref[:, D_FF + c * FF_TILE:D_FF + (c + 1) * FF_TILE])
        a_scr[0:halo, :] = jnp.where(seq_start, 0.0, tail_scr[:, cs])
        a_scr[halo:halo + tm, :] = a
        tail_scr[:, cs] = a[tm - halo:tm, :]
        conv = (cw_ref[0:1, cs] * a_scr[halo - 2:halo - 2 + tm, :]
                + cw_ref[1:2, cs] * a_scr[halo - 1:halo - 1 + tm, :]
                + cw_ref[2:3, cs] * a + cb_ref[:, cs])
        acc = acc + _dot((jax.nn.gelu(conv, approximate=True) * gate).astype(BF16), wd_ref[cs, :])
    o_ref[...] = x + gate_ref[0] * _rms_scale(acc, g2_ref[...])


def _conv_ffn(x2, g, sc, sh, w_up, conv_w, conv_b, w_down, g2, gate, seq):
    t, d = x2.shape
    tm = min(ROW_TILE, seq)
    per_seq = seq // tm
    row = lambda i: (i, 0)
    const = lambda i: (0, 0)
    by_batch = lambda i: (i // per_seq, 0, 0)
    return pl.pallas_call(
        functools.partial(_ffn_kernel, per_seq=per_seq),
        grid=(t // tm,),
        in_specs=[pl.BlockSpec((tm, d), row),
                  pl.BlockSpec((1, d), const),
                  pl.BlockSpec((1, 1, d), by_batch),
                  pl.BlockSpec((1, 1, d), by_batch),
                  _resident((d, 2 * D_FF), const),
                  pl.BlockSpec((CONV_W, D_FF), const),
                  pl.BlockSpec((1, D_FF), const),
                  _resident((D_FF, d), const),
                  pl.BlockSpec((1, d), const),
                  pl.BlockSpec((1, 1, d), by_batch)],
        out_specs=pl.BlockSpec((tm, d), row),
        out_shape=jax.ShapeDtypeStruct((t, d), F32),
        scratch_shapes=[pltpu.VMEM((tm + SUBLANES, FF_TILE), F32),
                        pltpu.VMEM((SUBLANES, D_FF), F32)],
        compiler_params=_params(1),
        name="conv_ffn",
    )(x2, g, sc, sh, w_up, conv_w, conv_b, w_down, g2, gate)


def kernel(x, c, w_ada, b_ada, norm_g, w_in, b_gate, rel_table, b_forget, w_branch, w_out,
           w_up, conv_w, conv_b, w_down):
    batch, seq, d = x.shape
    depth = w_ada.shape[0]
    t = batch * seq
    topk = min(TOPK_MAX, seq // 4)
    cos_t, sin_t = _rope_tables(seq)
    mod = _modulation(c, w_ada, b_ada)
    x2 = x.reshape(t, d)
    for l in range(depth):
        sh1, sc1, g1, sh2, sc2, g2 = (mod[l, :, n * d:(n + 1) * d].reshape(batch, 1, d)
                                      for n in range(6))
        oa, ob, oc, small, gates = _in_projection(
            x2, norm_g[l, 0:1], sc1, sh1, _layout_w_in(w_in[l]),
            b_gate[l].reshape(1, N_BRANCH * d), cos_t, sin_t, seq)

        ya = _band_attention(oa, _band_bias(rel_table[l]), batch, seq)

        f_rows = small[:, 0:HEADS].reshape(batch, seq, HEADS).transpose(0, 2, 1)
        planes = _forget_prefix(f_rows.reshape(batch * HEADS, seq),
                                jnp.tile(b_forget[l], batch).reshape(batch * HEADS, 1))
        yb = _fox_attention(*_fox_operands(ob, planes, batch, seq), batch, seq)

        yc = _sparse_attention(oc, *_sparse_operands(oc, small, batch, seq), batch, seq, topk)

        x2 = _merge_projection(ya, yb, yc, gates, x2, w_branch[l].astype(BF16),
                               w_out[l].astype(BF16), norm_g[l, 1:2], g1, seq)
        x2 = _conv_ffn(x2, norm_g[l, 2:3], sc2, sh2, w_up[l].astype(BF16), conv_w[l],
                       conv_b[l].reshape(1, D_FF), w_down[l].astype(BF16), norm_g[l, 3:4], g2, seq)
    return x2.reshape(batch, seq, d)
```

```python
import functools

import numpy as np
import jax
import jax.numpy as jnp
from jax import lax
from jax.experimental import pallas as pl
from jax.experimental.pallas import tpu as pltpu

F32 = jnp.float32
BF16 = jnp.bfloat16
I32 = jnp.int32

D_MODEL = 1024
CHUNK = 64
HEAD_DIM = 64
HEADS = 4
A_LEFT_CHUNKS = 8
A_MAX_REL = 128
IDX_DIM = 64
TOPK_MAX = 256
N_BRANCH = 3
MIX_W = 256
D_FF = 2816
CONV_W = 3
ROPE_THETA = 10000.0
EPS = 1e-6
NEG = -1e30

LANES = 128
SUBLANES = 8
VMEM_LIMIT = 56 * 1024 * 1024

_QA, _KA, _VA = 0, 512, 1024
_QB, _KB, _VB = 1280, 1536, 1792
_QC, _QI, _KK, _VC = 2048, 2560, 3072, 3200
_SM, _GT, _N_COLS = 3328, 3456, 6528

ROW_TILE = 512
A_TILE = 256
B_TILE = 256
B_CHUNK = 512
C_TILE = 256
C_KEYS = 128
C_CHUNK = 512
FF_TILE = 256

PACKED_ROWS = 16
INT16_MIN = -2 ** 15
I16 = jnp.int16


def _sortable_key_of(value):
    bits = int(np.array(value, np.float32).view(np.int32))
    return bits ^ ((bits >> 31) & 0x7FFFFFFF)


KEY_NEG = _sortable_key_of(NEG)


def _dot(a, b):
    return jnp.dot(a, b, preferred_element_type=F32)


def _dot_nt(a, b):
    return lax.dot_general(a, b, (((1,), (1,)), ((), ())), preferred_element_type=F32)


def _params(n_axes):
    return pltpu.CompilerParams(dimension_semantics=("arbitrary",) * n_axes,
                                vmem_limit_bytes=VMEM_LIMIT)


def _resident(shape, index_map):
    return pl.BlockSpec(shape, index_map, pipeline_mode=pl.Buffered(1))


def _mod_kernel(c_ref, w_ref, b_ref, o_ref):
    c = c_ref[...]
    c_act = c * jax.nn.sigmoid(c)
    o_ref[0] = _dot(c_act.astype(BF16), w_ref[0].astype(BF16)) + b_ref[0]


def _modulation(c, w_ada, b_ada):
    depth, d, n = w_ada.shape
    b = c.shape[0]
    return pl.pallas_call(
        _mod_kernel,
        grid=(depth, n // d),
        in_specs=[pl.BlockSpec((b, d), lambda l, j: (0, 0)),
                  pl.BlockSpec((1, d, d), lambda l, j: (l, 0, j)),
                  pl.BlockSpec((1, 1, d), lambda l, j: (l, 0, j))],
        out_specs=pl.BlockSpec((1, b, d), lambda l, j: (l, 0, j)),
        out_shape=jax.ShapeDtypeStruct((depth, b, n), F32),
        compiler_params=_params(2),
        name="adaln_mod",
    )(c, w_ada, b_ada.reshape(depth, 1, n))


def _modulated_norm(x, g, sc, sh):
    ms = jnp.mean(x * x, axis=-1, keepdims=True)
    return (x * lax.rsqrt(ms + EPS) * g) * (1.0 + sc) + sh


def _in_kernel(x_ref, g_ref, sc_ref, sh_ref, w_ref, bg_ref, cos_ref, sin_ref,
               oa_ref, ob_ref, oc_ref, os_ref, og_ref):
    hb = _modulated_norm(x_ref[...], g_ref[...], sc_ref[0], sh_ref[0]).astype(BF16)
    proj = lambda lo, hi: _dot(hb, w_ref[:, lo:hi])
    oa_ref[...] = proj(_QA, _QB).astype(BF16)
    ob_ref[...] = proj(_QB, _QC).astype(BF16)
    zr = proj(_QC, _VC)
    lane = lax.broadcasted_iota(I32, (zr.shape[0], LANES), 1)
    first_half = (lane % HEAD_DIM) < (HEAD_DIM // 2)
    cos = cos_ref[...]
    sin = sin_ref[...]
    for s in range((_VC - _QC) // LANES):
        xs = zr[:, s * LANES:(s + 1) * LANES]
        swapped = jnp.where(first_half, pltpu.roll(xs, LANES - HEAD_DIM // 2, 1),
                            pltpu.roll(xs, HEAD_DIM // 2, 1))
        oc_ref[:, s * LANES:(s + 1) * LANES] = (xs * cos + swapped * sin).astype(BF16)
    oc_ref[:, _VC - _QC:_SM - _QC] = proj(_VC, _SM).astype(BF16)
    os_ref[...] = proj(_SM, _GT)
    og_ref[...] = jax.nn.sigmoid(proj(_GT, _N_COLS) + bg_ref[...]).astype(BF16)


def _in_projection(x2, g, sc, sh, w, b_gate, cos_t, sin_t, seq):
    t, d = x2.shape
    tm = min(ROW_TILE, seq)
    per_seq = seq // tm
    row = lambda i: (i, 0)
    const = lambda i: (0, 0)
    by_batch = lambda i: (i // per_seq, 0, 0)
    by_pos = lambda i: (i % per_seq, 0)
    widths = (_QB - _QA, _QC - _QB, _SM - _QC, _GT - _SM, _N_COLS - _GT)
    dtypes = (BF16, BF16, BF16, F32, BF16)
    return pl.pallas_call(
        _in_kernel,
        grid=(t // tm,),
        in_specs=[pl.BlockSpec((tm, d), row),
                  pl.BlockSpec((1, d), const),
                  pl.BlockSpec((1, 1, d), by_batch),
                  pl.BlockSpec((1, 1, d), by_batch),
                  _resident((d, _N_COLS), const),
                  pl.BlockSpec((1, _N_COLS - _GT), const),
                  pl.BlockSpec((tm, LANES), by_pos),
                  pl.BlockSpec((tm, LANES), by_pos)],
        out_specs=[pl.BlockSpec((tm, wd), row) for wd in widths],
        out_shape=[jax.ShapeDtypeStruct((t, wd), dt) for wd, dt in zip(widths, dtypes)],
        compiler_params=_params(1),
        name="in_projection",
    )(x2, g, sc, sh, w, b_gate, cos_t, sin_t)


def _layout_w_in(w_in):
    sizes = (MIX_W,) * 6 + (HEADS, MIX_W, HEAD_DIM, HEAD_DIM, HEADS * IDX_DIM, IDX_DIM, HEADS,
                            N_BRANCH * D_MODEL)
    offs = np.cumsum((0,) + sizes)
    qa, ka, va, qb, kb, vb, fb, qc, kc, vc, qi, ki, wi, zg = (
        w_in[:, offs[n]:offs[n + 1]] for n in range(len(sizes)))
    d = w_in.shape[0]
    scale = HEAD_DIM ** -0.5

    def pad_heads(w, before=0):
        w = w.reshape(d, HEADS, HEAD_DIM)
        after = LANES - HEAD_DIM - before
        return jnp.pad(w, ((0, 0), (0, 0), (before, after))).reshape(d, HEADS * LANES)

    zeros = lambda n: jnp.zeros((d, n), w_in.dtype)
    cols = [pad_heads(qa * scale), pad_heads(ka), va,
            qb * scale, kb, vb,
            pad_heads(qc * scale), pad_heads(qi * (IDX_DIM ** -0.5), HEAD_DIM), kc, ki,
            vc, zeros(LANES - HEAD_DIM),
            fb, wi * (HEADS ** -0.5), zeros(LANES - 2 * HEADS),
            zg]
    w = jnp.concatenate(cols, axis=1)
    assert w.shape[1] == _N_COLS
    return w.astype(BF16)


def _rope_tables(seq):
    pos = jnp.arange(seq, dtype=F32)
    inv_freq = ROPE_THETA ** (-jnp.arange(0, HEAD_DIM, 2, dtype=F32) / HEAD_DIM)
    ang = pos[:, None] * inv_freq[None, :]
    cos, sin = jnp.cos(ang), jnp.sin(ang)
    reps = LANES // HEAD_DIM
    cos_t = jnp.tile(jnp.concatenate([cos, cos], axis=1), (1, reps))
    sin_t = jnp.tile(jnp.concatenate([-sin, sin], axis=1), (1, reps))
    return cos_t, sin_t


def _cum_kernel(f_ref, b_ref, o_ref):
    z = f_ref[...] + b_ref[...]
    c = jnp.minimum(z, 0.0) - jnp.log1p(jnp.exp(-jnp.abs(z)))
    n = c.shape[1]
    lane = lax.broadcasted_iota(I32, c.shape, 1)
    shift = 1
    while shift < n:
        c = c + jnp.where(lane >= shift, pltpu.roll(c, shift, 1), 0.0)
        shift *= 2
    hi = c.astype(BF16)
    r1 = c - hi.astype(F32)
    mid = r1.astype(BF16)
    lo = (r1 - mid.astype(F32)).astype(BF16)
    o_ref[0] = hi
    o_ref[1] = mid
    o_ref[2] = lo


def _forget_prefix(f_rows, b_col):
    r, s = f_rows.shape
    return pl.pallas_call(
        _cum_kernel,
        grid=(1,),
        in_specs=[pl.BlockSpec((r, s), lambda i: (0, 0)),
                  pl.BlockSpec((r, 1), lambda i: (0, 0))],
        out_specs=pl.BlockSpec((3, r, s), lambda i: (0, 0, 0)),
        out_shape=jax.ShapeDtypeStruct((3, r, s), BF16),
        compiler_params=_params(1),
        name="forget_prefix",
    )(f_rows, b_col)


def _merge_head_pairs(o_ref, outs):
    lane = lax.broadcasted_iota(I32, outs[0].shape, 1)
    for p in range(HEADS // 2):
        o_ref[:, p * LANES:(p + 1) * LANES] = jnp.where(
            lane < HEAD_DIM, outs[2 * p], outs[2 * p + 1]).astype(o_ref.dtype)


def _band_kernel(q_ref, k0_ref, k1_ref, k2_ref, v0_ref, v1_ref, v2_ref, bias_ref, o_ref):
    i = pl.program_id(1)
    k_refs = (k0_ref, k1_ref, k2_ref)
    v_refs = (v0_ref, v1_ref, v2_ref)
    n_kb = len(k_refs)
    outs = []
    for h in range(HEADS):
        hs = slice(h * LANES, (h + 1) * LANES)
        vs = slice((h // 2) * LANES, (h // 2 + 1) * LANES)
        q = q_ref[:, hs]
        s = []
        for j in range(n_kb):
            sj = _dot_nt(q, k_refs[j][:, hs]) + bias_ref[h, :, j * A_TILE:(j + 1) * A_TILE]
            if j < n_kb - 1:
                sj = jnp.where(i - (n_kb - 1) + j >= 0, sj, NEG)
            s.append(sj)
        m = functools.reduce(jnp.maximum, [sj.max(axis=1, keepdims=True) for sj in s])
        p = [jnp.exp(sj - m) for sj in s]
        l = functools.reduce(jnp.add, [pj.sum(axis=1, keepdims=True) for pj in p])
        acc = functools.reduce(jnp.add, [_dot(p[j].astype(BF16), v_refs[j][:, vs])
                                         for j in range(n_kb)])
        outs.append(acc / l)
    _merge_head_pairs(o_ref, outs)


def _band_bias(rel_table):
    n_kb = A_LEFT_CHUNKS * CHUNK // A_TILE + 1
    width = n_kb * A_TILE
    heads = rel_table.shape[0]
    span = A_TILE + width - 1
    top = (n_kb - 1) * A_TILE + A_TILE - 1
    n_hi = top - A_MAX_REL
    n_lo = span - n_hi - (2 * A_MAX_REL + 1)
    f = jnp.concatenate([jnp.repeat(rel_table[:, :1], n_lo, axis=1), rel_table,
                         jnp.repeat(rel_table[:, -1:], n_hi, axis=1)], axis=1)
    g = jnp.pad(f[:, ::-1], ((0, 0), (0, 1)))
    period = span + 1
    skew = jnp.tile(g, (1, A_TILE + 1))[:, :A_TILE * (period + 1)]
    skew = skew.reshape(heads, A_TILE, period + 1)[:, :, :width]
    bias = skew[:, ::-1, :].astype(F32)
    qi = np.arange(A_TILE)[:, None]
    kj = np.arange(width)[None, :]
    dc = qi // CHUNK + (n_kb - 1) * A_TILE // CHUNK - kj // CHUNK
    band = (dc >= 0) & (dc <= A_LEFT_CHUNKS)
    return jnp.where(jnp.asarray(band)[None], bias, NEG)


def _band_attention(oa, bias, batch, seq):
    t = oa.shape[0]
    nq = seq // A_TILE
    n_kb = bias.shape[2] // A_TILE
    qw = HEADS * LANES

    def k_spec(j, width, col):
        return pl.BlockSpec(
            (A_TILE, width),
            lambda b, i: (b * nq + jnp.maximum(i - (n_kb - 1) + j, 0), col))

    return pl.pallas_call(
        _band_kernel,
        grid=(batch, nq),
        in_specs=([pl.BlockSpec((A_TILE, qw), lambda b, i: (b * nq + i, _QA // qw))]
                  + [k_spec(j, qw, _KA // qw) for j in range(n_kb)]
                  + [k_spec(j, MIX_W, _VA // MIX_W) for j in range(n_kb)]
                  + [pl.BlockSpec(bias.shape, lambda b, i: (0, 0, 0))]),
        out_specs=pl.BlockSpec((A_TILE, MIX_W), lambda b, i: (b * nq + i, 0)),
        out_shape=jax.ShapeDtypeStruct((t, MIX_W), BF16),
        compiler_params=_params(2),
        name="band_attention",
    )(oa, *([oa] * (2 * n_kb)), bias)


def _fold_rows(x, op):
    return op(x.reshape(x.shape[0] // SUBLANES, SUBLANES, x.shape[1]), axis=0)


def _fox_kernel(q_ref, k_ref, vt_ref, o_ref, s_scr):
    i = pl.program_id(1)
    tq = q_ref.shape[0]
    n_chunks = i // (B_CHUNK // B_TILE) + 1
    key_row = lax.broadcasted_iota(I32, (B_CHUNK, tq), 0)
    query_col = lax.broadcasted_iota(I32, (B_CHUNK, tq), 1)
    chunk_rows = lambda c: pl.ds(pl.multiple_of(c * B_CHUNK, B_CHUNK), B_CHUNK)
    lanes_of = lambda h: slice(h * LANES, (h + 1) * LANES)
    cols_of = lambda h: slice(h * tq, (h + 1) * tq)

    def scores(c, mx, causal):
        rows = chunk_rows(c)
        new = []
        for h in range(HEADS):
            s = _dot_nt(k_ref[rows, lanes_of(h)], q_ref[:, lanes_of(h)])
            if causal:
                s = jnp.where(c * B_CHUNK + key_row <= i * tq + query_col, s, NEG)
            s_scr[rows, cols_of(h)] = s
            new.append(jnp.maximum(mx[h], _fold_rows(s, jnp.max)))
        return tuple(new)

    mx = tuple(jnp.full((SUBLANES, tq), NEG, F32) for _ in range(HEADS))
    mx = lax.fori_loop(0, n_chunks - 1, lambda c, mx: scores(c, mx, False), mx)
    mx = scores(n_chunks - 1, mx, True)
    m = [jnp.max(mx[h], axis=0, keepdims=True) for h in range(HEADS)]

    def attend(c, carry):
        rows = chunk_rows(c)
        v_t = vt_ref[0, c]
        new = []
        for h in range(HEADS):
            l, acc = carry[h]
            p = jnp.exp(s_scr[rows, cols_of(h)] - m[h])
            new.append((l + _fold_rows(p, jnp.sum),
                        acc + _dot(v_t[h * HEAD_DIM:(h + 1) * HEAD_DIM, :], p.astype(BF16))))
        return tuple(new)

    init = tuple((jnp.zeros((SUBLANES, tq), F32), jnp.zeros((HEAD_DIM, tq), F32))
                 for _ in range(HEADS))
    final = lax.fori_loop(0, n_chunks, attend, init)
    out_t = jnp.concatenate([acc / jnp.sum(l, axis=0, keepdims=True) for l, acc in final],
                            axis=0)
    o_ref[...] = out_t.T.astype(o_ref.dtype)


def _fox_attention(q_aug, k_aug, v_t, batch, seq):
    t = q_aug.shape[0]
    nq = seq // B_TILE
    qw = HEADS * LANES
    return pl.pallas_call(
        _fox_kernel,
        grid=(batch, nq),
        in_specs=[pl.BlockSpec((B_TILE, qw), lambda b, i: (b * nq + i, 0)),
                  pl.BlockSpec((seq, qw), lambda b, i: (b, 0)),
                  pl.BlockSpec((1, seq // B_CHUNK, MIX_W, B_CHUNK), lambda b, i: (b, 0, 0, 0))],
        out_specs=pl.BlockSpec((B_TILE, MIX_W), lambda b, i: (b * nq + i, 0)),
        out_shape=jax.ShapeDtypeStruct((t, MIX_W), BF16),
        scratch_shapes=[pltpu.VMEM((seq, HEADS * B_TILE), F32)],
        compiler_params=_params(2),
        name="fox_attention",
    )(q_aug, k_aug, v_t)


def _fox_operands(ob, cum_planes, batch, seq):
    t = batch * seq
    planes = cum_planes.reshape(3, batch, HEADS, seq).transpose(1, 3, 2, 0).reshape(t, HEADS, 3)
    ones = jnp.ones_like(planes)
    fill = jnp.zeros((t, HEADS, LANES - HEAD_DIM - 6), BF16)
    q = ob[:, 0:MIX_W].reshape(t, HEADS, HEAD_DIM)
    k = ob[:, MIX_W:2 * MIX_W].reshape(t, HEADS, HEAD_DIM)
    q_aug = jnp.concatenate([q, planes, ones, fill], axis=-1).reshape(t, HEADS * LANES)
    k_aug = jnp.concatenate([k, ones, -planes, fill], axis=-1).reshape(t, HEADS * LANES)
    v_t = ob[:, 2 * MIX_W:3 * MIX_W].reshape(batch, seq // B_CHUNK, B_CHUNK, MIX_W)
    return q_aug, k_aug, v_t.transpose(0, 1, 3, 2)


def _sparse_kernel(kk_ref, wi_ref, wc_ref, w_ref, vt_ref, o_ref,
                   key_scr, hi_scr, lo_scr, s_scr, *, topk):
    i = pl.program_id(1)
    kb = C_KEYS
    nq = C_TILE
    blocks = C_CHUNK // kb
    n_chunks = (i * nq + nq - 1) // C_CHUNK + 1
    row = lax.broadcasted_iota(I32, (kb, nq), 0)
    lane = lax.broadcasted_iota(I32, (kb, nq), 1)
    limit = i * nq + CHUNK * (lane // CHUNK + 1)
    w = w_ref[0, 0]
    block_rows = lambda c, u: pl.ds(pl.multiple_of(c * C_CHUNK + u * kb, kb), kb)
    chunk_rows = lambda c: pl.ds(pl.multiple_of(c * C_CHUNK, C_CHUNK), C_CHUNK)
    head = lambda h: slice(h * nq, (h + 1) * nq)
    lanes_of = lambda h: slice(h * LANES, (h + 1) * LANES)

    def score_chunk(c, carry):
        for u in range(blocks):
            rows = block_rows(c, u)
            keys = kk_ref[rows, :]
            score = functools.reduce(jnp.add, [
                w[h:h + 1, :] * jnp.maximum(_dot_nt(keys, wi_ref[:, lanes_of(h)]), 0.0)
                for h in range(HEADS)])
            bits = pltpu.bitcast(score + 0.0, I32)
            key = bits ^ ((bits >> 31) & 0x7FFFFFFF)
            key = jnp.where(c * C_CHUNK + u * kb + row < limit, key, KEY_NEG)
            key_scr[rows, :] = key
            hi_scr[rows, :] = (key >> 16).astype(I16)
            lo_scr[rows, :] = ((key & 0xFFFF) + INT16_MIN).astype(I16)
        return carry

    lax.fori_loop(0, n_chunks, score_chunk, 0)

    def count16(scr, pred):
        def body(c, acc):
            vals = scr[chunk_rows(c), :]
            parts = [jnp.where(pred(vals[g * PACKED_ROWS:(g + 1) * PACKED_ROWS, :]),
                               jnp.int16(1), jnp.int16(0))
                     for g in range(C_CHUNK // PACKED_ROWS)]
            while len(parts) > 1:
                parts = [a + b for a, b in zip(parts[0::2], parts[1::2])]
            return acc + parts[0]
        acc = lax.fori_loop(0, n_chunks, body, jnp.zeros((PACKED_ROWS, nq), I16))
        return jnp.sum(acc.astype(I32), axis=0, keepdims=True)

    def search16(scr, need):
        def bit_step(it, thr):
            cand = thr + lax.shift_left(jnp.int32(1), 15 - it)
            cand16 = cand.astype(I16)
            return jnp.where(count16(scr, lambda v: v >= cand16) >= need, cand, thr)
        return lax.fori_loop(0, 16, bit_step, jnp.full((1, nq), INT16_MIN, I32))

    t_hi = search16(hi_scr, topk)
    t_hi16 = t_hi.astype(I16)
    room_hi = topk - count16(hi_scr, lambda v: v > t_hi16)

    def bucket_chunk(c, carry):
        rows = chunk_rows(c)
        lo_scr[rows, :] = jnp.where(hi_scr[rows, :] == t_hi16, lo_scr[rows, :],
                                    jnp.int16(INT16_MIN))
        return carry

    lax.fori_loop(0, n_chunks, bucket_chunk, 0)
    t_lo = search16(lo_scr, room_hi)
    thr = t_hi * 65536 + (t_lo - INT16_MIN)

    t_lo16 = t_lo.astype(I16)
    room = (room_hi - count16(lo_scr, lambda v: v > t_lo16)).astype(F32)
    strict_lower = (lax.broadcasted_iota(I32, (kb, kb), 1)
                    < lax.broadcasted_iota(I32, (kb, kb), 0)).astype(BF16)

    def select_chunk(c, carry):
        mx, ties_before = list(carry[:HEADS]), carry[HEADS]
        for u in range(blocks):
            rows = block_rows(c, u)
            key = key_scr[rows, :]
            tie = jnp.where(key == thr, 1.0, 0.0)
            rank = _dot(strict_lower, tie.astype(BF16)) + ties_before
            sel = ((c * C_CHUNK + u * kb + row < limit)
                   & ((key > thr) | ((key == thr) & (rank < room))))
            ties_before = ties_before + jnp.sum(tie, axis=0, keepdims=True)
            keys = kk_ref[rows, :]
            for h in range(HEADS):
                s = jnp.where(sel, _dot_nt(keys, wc_ref[:, lanes_of(h)]), NEG)
                s_scr[rows, head(h)] = s
                mx[h] = jnp.maximum(mx[h], _fold_rows(s, jnp.max))
        return (*mx, ties_before)

    done = lax.fori_loop(0, n_chunks, select_chunk,
                         (*[jnp.full((SUBLANES, nq), NEG, F32) for _ in range(HEADS)],
                          jnp.zeros((1, nq), F32)))

    m = [jnp.max(done[h], axis=0, keepdims=True) for h in range(HEADS)]

    def attend_chunk(c, carry):
        rows = chunk_rows(c)
        v_t = vt_ref[0, c]
        new = []
        for h in range(HEADS):
            l, acc = carry[h]
            p = jnp.exp(s_scr[rows, head(h)] - m[h])
            new.append((l + _fold_rows(p, jnp.sum), acc + _dot(v_t, p.astype(BF16))))
        return tuple(new)

    init = tuple((jnp.zeros((SUBLANES, nq), F32), jnp.zeros((HEAD_DIM, nq), F32))
                 for _ in range(HEADS))
    final = lax.fori_loop(0, n_chunks, attend_chunk, init)
    out_t = jnp.concatenate([acc / jnp.sum(l, axis=0, keepdims=True) for l, acc in final],
                            axis=0)
    o_ref[...] = out_t.T.astype(o_ref.dtype)


def _sparse_attention(oc, w_rows, v_t, batch, seq, topk):
    t = oc.shape[0]
    nq = seq // C_TILE
    qw = HEADS * LANES
    return pl.pallas_call(
        functools.partial(_sparse_kernel, topk=topk),
        grid=(batch, nq),
        in_specs=[pl.BlockSpec((seq, LANES), lambda b, i: (b, (_KK - _QC) // LANES)),
                  pl.BlockSpec((C_TILE, qw), lambda b, i: (b * nq + i, (_QI - _QC) // qw)),
                  pl.BlockSpec((C_TILE, qw), lambda b, i: (b * nq + i, 0)),
                  pl.BlockSpec((1, 1, HEADS, C_TILE), lambda b, i: (b, i, 0, 0)),
                  pl.BlockSpec((1, seq // C_CHUNK, HEAD_DIM, C_CHUNK), lambda b, i: (b, 0, 0, 0))],
        out_specs=pl.BlockSpec((C_TILE, MIX_W), lambda b, i: (b * nq + i, 0)),
        out_shape=jax.ShapeDtypeStruct((t, MIX_W), BF16),
        scratch_shapes=[pltpu.VMEM((seq, C_TILE), I32),
                        pltpu.VMEM((seq, C_TILE), I16),
                        pltpu.VMEM((seq, C_TILE), I16),
                        pltpu.VMEM((seq, HEADS * C_TILE), F32)],
        compiler_params=_params(2),
        name="sparse_attention",
    )(oc, oc, oc, w_rows, v_t)


def _sparse_operands(oc, small, batch, seq):
    nq = seq // C_TILE
    w_rows = small[:, HEADS:2 * HEADS].reshape(batch, nq, C_TILE, HEADS).transpose(0, 1, 3, 2)
    v = oc[:, _VC - _QC:_VC - _QC + HEAD_DIM]
    v_t = v.reshape(batch, seq // C_CHUNK, C_CHUNK, HEAD_DIM).transpose(0, 1, 3, 2)
    return w_rows, v_t


def _rms_scale(y, g):
    return y * lax.rsqrt(jnp.mean(y * y, axis=-1, keepdims=True) + EPS) * g


def _merge_kernel(ya_ref, yb_ref, yc_ref, gt_ref, x_ref, wb_ref, wo_ref, g_ref, gate_ref, o_ref):
    d = x_ref.shape[1]
    merged = functools.reduce(jnp.add, [
        gt_ref[:, n * d:(n + 1) * d].astype(F32) * _dot(y_ref[...], wb_ref[n])
        for n, y_ref in enumerate((ya_ref, yb_ref, yc_ref))])
    y = _dot(merged.astype(BF16), wo_ref[...])
    o_ref[...] = x_ref[...] + gate_ref[0] * _rms_scale(y, g_ref[...])


def _merge_projection(ya, yb, yc, gates, x2, w_branch, w_out, g, gate, seq):
    t, d = x2.shape
    tm = min(ROW_TILE, seq)
    per_seq = seq // tm
    row = lambda i: (i, 0)
    return pl.pallas_call(
        _merge_kernel,
        grid=(t // tm,),
        in_specs=[pl.BlockSpec((tm, MIX_W), row)] * 3
        + [pl.BlockSpec((tm, N_BRANCH * d), row),
           pl.BlockSpec((tm, d), row),
           _resident((N_BRANCH, MIX_W, d), lambda i: (0, 0, 0)),
           _resident((d, d), lambda i: (0, 0)),
           pl.BlockSpec((1, d), lambda i: (0, 0)),
           pl.BlockSpec((1, 1, d), lambda i: (i // per_seq, 0, 0))],
        out_specs=pl.BlockSpec((tm, d), row),
        out_shape=jax.ShapeDtypeStruct((t, d), F32),
        compiler_params=_params(1),
        name="merge_projection",
    )(ya, yb, yc, gates, x2, w_branch, w_out, g, gate)


def _ffn_kernel(x_ref, g_ref, sc_ref, sh_ref, wu_ref, cw_ref, cb_ref, wd_ref, g2_ref, gate_ref,
                o_ref, a_scr, tail_scr, *, per_seq):
    i = pl.program_id(0)
    tm = x_ref.shape[0]
    halo = SUBLANES
    x = x_ref[...]
    hb = _modulated_norm(x, g_ref[...], sc_ref[0], sh_ref[0]).astype(BF16)
    seq_start = (i % per_seq) == 0
    acc = jnp.zeros(x.shape, F32)
    for c in range(D_FF // FF_TILE):
        cs = slice(c * FF_TILE, (c + 1) * FF_TILE)
        a = _dot(hb, wu_ref[:, cs])
        gate = _dot(hb, wu_ref[:, D_FF + c * FF_TILE:D_FF + (c + 1) * FF_TILE])
        a_scr[0:halo, :] = jnp.where(seq_start, 0.0, tail_scr[:, cs])
        a_scr[halo:halo + tm, :] = a
        tail_scr[:, cs] = a[tm - halo:tm, :]
        conv = (cw_ref[0:1, cs] * a_scr[halo - 2:halo - 2 + tm, :]
                + cw_ref[1:2, cs] * a_scr[halo - 1:halo - 1 + tm, :]
                + cw_ref[2:3, cs] * a + cb_ref[:, cs])
        acc = acc + _dot((jax.nn.gelu(conv, approximate=True) * gate).astype(BF16), wd_ref[cs, :])
    o_ref[...] = x + gate_ref[0] * _rms_scale(acc, g2_ref[...])


def _conv_ffn(x2, g, sc, sh, w_up, conv_w, conv_b, w_down, g2, gate, seq):
    t, d = x2.shape
    tm = min(ROW_TILE, seq)
    per_seq = seq // tm
    row = lambda i: (i, 0)
    const = lambda i: (0, 0)
    by_batch = lambda i: (i // per_seq, 0, 0)
    return pl.pallas_call(
        functools.partial(_ffn_kernel, per_seq=per_seq),
        grid=(t // tm,),
        in_specs=[pl.BlockSpec((tm, d), row),
                  pl.BlockSpec((1, d), const),
                  pl.BlockSpec((1, 1, d), by_batch),
                  pl.BlockSpec((1, 1, d), by_batch),
                  _resident((d, 2 * D_FF), const),
                  pl.BlockSpec((CONV_W, D_FF), const),
                  pl.BlockSpec((1, D_FF), const),
                  _resident((D_FF, d), const),
                  pl.BlockSpec((1, d), const),
                  pl.BlockSpec((1, 1, d), by_batch)],
        out_specs=pl.BlockSpec((tm, d), row),
        out_shape=jax.ShapeDtypeStruct((t, d), F32),
        scratch_shapes=[pltpu.VMEM((tm + SUBLANES, FF_TILE), F32),
                        pltpu.VMEM((SUBLANES, D_FF), F32)],
        compiler_params=_params(1),
        name="conv_ffn",
    )(x2, g, sc, sh, w_up, conv_w, conv_b, w_down, g2, gate)


def kernel(x, c, w_ada, b_ada, norm_g, w_in, b_gate, rel_table, b_forget, w_branch, w_out,
           w_up, conv_w, conv_b, w_down):
    batch, seq, d = x.shape
    depth = w_ada.shape[0]
    t = batch * seq
    topk = min(TOPK_MAX, seq // 4)
    cos_t, sin_t = _rope_tables(seq)
    mod = _modulation(c, w_ada, b_ada)
    x2 = x.reshape(t, d)
    for l in range(depth):
        sh1, sc1, g1, sh2, sc2, g2 = (mod[l, :, n * d:(n + 1) * d].reshape(batch, 1, d)
                                      for n in range(6))
        oa, ob, oc, small, gates = _in_projection(
            x2, norm_g[l, 0:1], sc1, sh1, _layout_w_in(w_in[l]),
            b_gate[l].reshape(1, N_BRANCH * d), cos_t, sin_t, seq)

        ya = _band_attention(oa, _band_bias(rel_table[l]), batch, seq)

        f_rows = small[:, 0:HEADS].reshape(batch, seq, HEADS).transpose(0, 2, 1)
        planes = _forget_prefix(f_rows.reshape(batch * HEADS, seq),
                                jnp.tile(b_forget[l], batch).reshape(batch * HEADS, 1))
        yb = _fox_attention(*_fox_operands(ob, planes, batch, seq), batch, seq)

        yc = _sparse_attention(oc, *_sparse_operands(oc, small, batch, seq), batch, seq, topk)

        x2 = _merge_projection(ya, yb, yc, gates, x2, w_branch[l].astype(BF16),
                               w_out[l].astype(BF16), norm_g[l, 1:2], g1, seq)
        x2 = _conv_ffn(x2, norm_g[l, 2:3], sc2, sh2, w_up[l].astype(BF16), conv_w[l],
                       conv_b[l].reshape(1, D_FF), w_down[l].astype(BF16), norm_g[l, 3:4], g2, seq)
    return x2.reshape(batch, seq, d)
```
